```python
import jax, jax.numpy as jnp
from jax import lax
import numpy as np

D_MODEL = 1024
BATCH = 2
SEQ = 8192
DEPTH = 4
DEC_BATCH = 32
DEC_SEQ = 4
PAST_LEN = 8192
PAGE_SIZE = 128

N_REC = (DEPTH + 1) // 2
N_ATT = DEPTH // 2
CONV_W = 4
D_RNN = D_MODEL // 2
RG_HEADS = 8
RG_BW = D_RNN // RG_HEADS
RG_C = 8.0
DN_HEADS = 4
DN_DK = 128
DN_DV = 128
DN_QK = DN_HEADS * DN_DK
DN_VD = DN_HEADS * DN_DV
DN_CHUNK = 64
CONV_CH = D_RNN + 2 * DN_QK + DN_VD
REC_IN = CONV_CH + D_RNN + DN_VD + 2 * DN_HEADS
ATT_HEADS = D_MODEL // 128
KV_HEADS = 2
HEAD_DIM = 128
GROUP = ATT_HEADS // KV_HEADS
IDX_HEADS = 8
IDX_DIM = 64
TOPK_MAX = 256
Q_BLOCK = 128
ATT_IN = ATT_HEADS * HEAD_DIM + 2 * KV_HEADS * HEAD_DIM + IDX_HEADS * IDX_DIM + IDX_DIM + IDX_HEADS
D_FF = 4 * D_MODEL
EPS = 1e-6

kernel_name = 'hybrid_rglru_gdn_dsa_step'


def rmsnorm(x, g):
    x32 = x.astype(jnp.float32)
    y = x32 * lax.rsqrt(jnp.mean(x32 * x32, axis=-1, keepdims=True) + EPS)
    return (y * g.astype(jnp.float32)).astype(x.dtype)


def l2norm(x):
    return x * lax.rsqrt(jnp.sum(x * x, axis=-1, keepdims=True) + EPS)


def mlp(x, w_up, w_down):
    h = jax.nn.relu(x @ w_up)
    return (h * h) @ w_down


def causal_dwconv(u, buf, w):
    T = u.shape[1]
    xp = jnp.concatenate([buf.astype(u.dtype), u], axis=1)
    out = xp[:, 0:T] * w[0]
    for i in range(1, CONV_W):
        out = out + xp[:, i:i + T] * w[i]
    return out, xp[:, T:]


def rglru(xr, is_first, h0, wa, ba, wx, bx, lam):
    B, T, _ = xr.shape
    f32 = jnp.float32
    x32 = xr.astype(f32)
    xb = x32.reshape(B, T, RG_HEADS, RG_BW)
    r = jax.nn.sigmoid(jnp.einsum('bthi,hij->bthj', xb, wa.astype(f32)).reshape(B, T, D_RNN) + ba.astype(f32))
    ig = jax.nn.sigmoid(jnp.einsum('bthi,hij->bthj', xb, wx.astype(f32)).reshape(B, T, D_RNN) + bx.astype(f32))
    log_a = -RG_C * r * jax.nn.softplus(-lam.astype(f32))
    a = jnp.exp(log_a)
    mult = jnp.where(is_first[None, :, None], 1.0, jnp.sqrt(jnp.maximum(1.0 - jnp.exp(2.0 * log_a), 0.0)))
    b = x32 * ig * mult
    b = b.at[:, 0].add(a[:, 0] * h0.astype(f32))

    def comb(lhs, rhs):
        return (lhs[0] * rhs[0], rhs[0] * lhs[1] + rhs[1])

    _, h = lax.associative_scan(comb, (a, b), axis=1)
    return h, h[:, -1]


def gated_delta(q, k, v, g, beta, S0):
    B, T, H, DK = q.shape
    DV = v.shape[-1]
    C = DN_CHUNK if T % DN_CHUNK == 0 else T
    n = T // C
    f32 = jnp.float32

    def chunks(x):
        return jnp.moveaxis(x.reshape((B, n, C, H) + x.shape[3:]), 3, 1)

    q = chunks(l2norm(q.astype(f32)) * DK ** -0.5)
    k = chunks(l2norm(k.astype(f32)))
    v = chunks(v.astype(f32))
    g = chunks(g.astype(f32))
    beta = chunks(beta.astype(f32))
    gc = jnp.cumsum(g, axis=-1)
    tri = jnp.tril(jnp.ones((C, C), bool))
    strict = jnp.tril(jnp.ones((C, C), bool), -1)
    decay = jnp.exp(jnp.where(tri, gc[..., :, None] - gc[..., None, :], -jnp.inf))
    kb = k * beta[..., None]
    lmat = jnp.where(strict, jnp.einsum('bhnid,bhnjd->bhnij', kb, k) * decay, 0.0)
    amat = lmat + jnp.eye(C, dtype=f32)
    rhs = jnp.concatenate([v * beta[..., None], kb * jnp.exp(gc)[..., None]], axis=-1)
    sol = lax.linalg.triangular_solve(amat, rhs, left_side=True, lower=True, unit_diagonal=True)
    u, w = sol[..., :DV], sol[..., DV:]
    attn = jnp.einsum('bhnid,bhnjd->bhnij', q, k) * decay
    qg = q * jnp.exp(gc)[..., None]
    kd = k * jnp.exp(gc[..., -1:] - gc)[..., None]
    glast = jnp.exp(gc[..., -1])

    def step(S, inp):
        u_i, w_i, a_i, qg_i, kd_i, gl_i = inp
        vnew = u_i - jnp.einsum('bhcd,bhdv->bhcv', w_i, S)
        o = jnp.einsum('bhcd,bhdv->bhcv', qg_i, S) + jnp.einsum('bhij,bhjv->bhiv', a_i, vnew)
        S = S * gl_i[..., None, None] + jnp.einsum('bhcd,bhcv->bhdv', kd_i, vnew)
        return S, o

    xs = tuple(jnp.moveaxis(t, 2, 0) for t in (u, w, attn, qg, kd, glast))
    S_T, o = lax.scan(step, S0.astype(f32), xs)
    o = jnp.moveaxis(jnp.moveaxis(o, 0, 2), 1, 3).reshape(B, T, H, DV)
    return o, S_T


def rec_mixer(xn, conv_buf, h0, S0, is_first, w_in, w_conv, b_conv, wa, ba, wx, bx, lam,
              a_log, dt_bias, dn_w, w_out):
    B, T, _ = xn.shape
    p = xn @ w_in
    u = p[..., :CONV_CH]
    gate = p[..., CONV_CH:CONV_CH + D_RNN]
    z = p[..., CONV_CH + D_RNN:CONV_CH + D_RNN + DN_VD]
    off = CONV_CH + D_RNN + DN_VD
    b_raw = p[..., off:off + DN_HEADS]
    a_raw = p[..., off + DN_HEADS:off + 2 * DN_HEADS]
    uc, new_buf = causal_dwconv(u, conv_buf, w_conv)
    xr = uc[..., :D_RNN] + b_conv
    h, hT = rglru(xr, is_first, h0, wa, ba, wx, bx, lam)
    rg_y = (h * jax.nn.gelu(gate.astype(jnp.float32))).astype(xn.dtype)
    qkv = jax.nn.silu(uc[..., D_RNN:])
    q = qkv[..., :DN_QK].reshape(B, T, DN_HEADS, DN_DK)
    k = qkv[..., DN_QK:2 * DN_QK].reshape(B, T, DN_HEADS, DN_DK)
    v = qkv[..., 2 * DN_QK:].reshape(B, T, DN_HEADS, DN_DV)
    beta = jax.nn.sigmoid(b_raw.astype(jnp.float32))
    g = -jnp.exp(a_log.astype(jnp.float32)) * jax.nn.softplus(a_raw.astype(jnp.float32) + dt_bias.astype(jnp.float32))
    o, S_T = gated_delta(q, k, v, g, beta, S0)
    o = rmsnorm(o, dn_w) * jax.nn.silu(z.astype(jnp.float32).reshape(B, T, DN_HEADS, DN_DV))
    dn_y = o.reshape(B, T, DN_VD).astype(xn.dtype)
    y = jnp.concatenate([rg_y, dn_y], axis=-1) @ w_out
    return y, new_buf, hT.astype(xn.dtype), S_T.astype(xn.dtype)


def att_project(xn, w_in):
    B, T, _ = xn.shape
    p = xn @ w_in
    o1 = ATT_HEADS * HEAD_DIM
    o2 = o1 + KV_HEADS * HEAD_DIM
    o3 = o2 + KV_HEADS * HEAD_DIM
    o4 = o3 + IDX_HEADS * IDX_DIM
    o5 = o4 + IDX_DIM
    q = p[..., :o1].reshape(B, T, ATT_HEADS, HEAD_DIM)
    k = p[..., o1:o2].reshape(B, T, KV_HEADS, HEAD_DIM)
    v = p[..., o2:o3].reshape(B, T, KV_HEADS, HEAD_DIM)
    qi = p[..., o3:o4].reshape(B, T, IDX_HEADS, IDX_DIM)
    ki = p[..., o4:o5]
    wi = p[..., o5:]
    return q, k, v, qi, ki, wi


def index_scores(qi, wi, ki):
    s = jax.nn.relu(jnp.einsum('bthd,bsd->bths', qi, ki).astype(jnp.float32))
    return jnp.einsum('bth,bths->bts', wi.astype(jnp.float32), s)


def sparse_attend(q, kg, vg, valid):
    B, T = q.shape[:2]
    qg = q.reshape(B, T, KV_HEADS, GROUP, HEAD_DIM)
    s = jnp.einsum('btkgd,btjkd->btkgj', qg, kg).astype(jnp.float32) * HEAD_DIM ** -0.5
    s = jnp.where(valid[:, :, None, None, :], s, -1e30)
    pr = jax.nn.softmax(s, axis=-1)
    o = jnp.einsum('btkgj,btjkd->btkgd', pr.astype(vg.dtype), vg)
    return o.reshape(B, T, ATT_HEADS * HEAD_DIM)


def dsa_prompt(xn, w_in, w_out):
    B, S, _ = xn.shape
    q, k, v, qi, ki, wi = att_project(xn, w_in)
    topk = min(TOPK_MAX, S // 4)
    qb = min(Q_BLOCK, S)
    nb = S // qb
    bidx = jnp.arange(B)[:, None, None]
    kpos = jnp.arange(S)

    def blocks(x):
        return jnp.moveaxis(x.reshape((B, nb, qb) + x.shape[2:]), 1, 0)

    def blk(args):
        q_b, qi_b, wi_b, bi = args
        tpos = bi * qb + jnp.arange(qb)
        sc = index_scores(qi_b, wi_b, ki)
        sc = jnp.where(kpos[None, None, :] <= tpos[None, :, None], sc, -jnp.inf)
        _, idx = lax.top_k(sc, topk)
        valid = idx <= tpos[None, :, None]
        return sparse_attend(q_b, k[bidx, idx], v[bidx, idx], valid)

    o = lax.map(blk, (blocks(q), blocks(qi), blocks(wi), jnp.arange(nb)))
    o = jnp.moveaxis(o, 0, 1).reshape(B, S, ATT_HEADS * HEAD_DIM)
    return o @ w_out, k, v, ki


def dsa_sample(xn, ck, cv, cki, page_table, w_in, w_out):
    DB, T, _ = xn.shape
    page = ck.shape[1]
    past = page_table.shape[1] * page
    L = past + T
    q, k, v, qi, ki, wi = att_project(xn, w_in)
    ki_past = cki[page_table].reshape(DB, past, IDX_DIM)
    ki_all = jnp.concatenate([ki_past, ki.astype(ki_past.dtype)], axis=1)
    tpos = past + jnp.arange(T)
    sc = index_scores(qi, wi, ki_all)
    sc = jnp.where(jnp.arange(L)[None, None, :] <= tpos[None, :, None], sc, -jnp.inf)
    _, idx = lax.top_k(sc, min(TOPK_MAX, L // 4))
    bidx = jnp.arange(DB)[:, None, None]
    pidx = jnp.minimum(idx, past - 1)
    phys = page_table[bidx, pidx // page] * page + pidx % page
    nidx = jnp.clip(idx - past, 0, T - 1)
    is_new = (idx >= past)[..., None, None]
    ck_rows = ck.reshape((-1,) + ck.shape[2:])
    cv_rows = cv.reshape((-1,) + cv.shape[2:])
    kg = jnp.where(is_new, k[bidx, nidx].astype(ck.dtype), ck_rows[phys])
    vg = jnp.where(is_new, v[bidx, nidx].astype(cv.dtype), cv_rows[phys])
    valid = idx <= tpos[None, :, None]
    o = sparse_attend(q, kg, vg, valid)
    return o @ w_out, k, v, ki


def setup_inputs(seed: int = 0) -> dict:
    key = jax.random.key(seed)
    ks = jax.random.split(key, 32)
    f32 = jnp.float32
    n_pages = PAST_LEN // PAGE_SIZE
    n_used = DEC_BATCH * n_pages
    n_pool = n_used + (n_used + 3) // 4

    def nrm(k, shape, s):
        return jax.random.normal(k, shape, f32) * s

    a0 = jax.random.uniform(ks[15], (N_REC, D_RNN), f32, 0.9, 0.999) ** (1.0 / RG_C)
    return {
        'x_prompt': nrm(ks[0], (BATCH, SEQ, D_MODEL), 1.0),
        'x_sample': nrm(ks[1], (DEC_BATCH, DEC_SEQ, D_MODEL), 1.0),
        'state_conv': nrm(ks[2], (N_REC, DEC_BATCH, CONV_W - 1, CONV_CH), 1.0),
        'state_h': nrm(ks[3], (N_REC, DEC_BATCH, D_RNN), 0.5),
        'state_S': nrm(ks[4], (N_REC, DEC_BATCH, DN_HEADS, DN_DK, DN_DV), DN_DK ** -0.5),
        'cache_k': nrm(ks[5], (N_ATT, n_pool, PAGE_SIZE, KV_HEADS, HEAD_DIM), 1.0),
        'cache_v': nrm(ks[6], (N_ATT, n_pool, PAGE_SIZE, KV_HEADS, HEAD_DIM), 1.0),
        'cache_kidx': nrm(ks[7], (N_ATT, n_pool, PAGE_SIZE, IDX_DIM), 1.0),
        'page_table': jax.random.permutation(ks[8], n_pool)[:n_used].reshape(DEC_BATCH, n_pages).astype(jnp.int32),
        'norm_mix': 1.0 + nrm(ks[9], (DEPTH, D_MODEL), 0.01),
        'norm_mlp': 1.0 + nrm(ks[10], (DEPTH, D_MODEL), 0.01),
        'norm_final': 1.0 + nrm(ks[11], (D_MODEL,), 0.01),
        'w_in_rec': nrm(ks[12], (N_REC, D_MODEL, REC_IN), D_MODEL ** -0.5),
        'w_conv': nrm(ks[13], (N_REC, CONV_W, CONV_CH), CONV_W ** -0.5),
        'b_conv': nrm(ks[14], (N_REC, D_RNN), 0.01),
        'rg_wa': nrm(ks[16], (N_REC, RG_HEADS, RG_BW, RG_BW), RG_BW ** -0.5),
        'rg_ba': nrm(ks[17], (N_REC, D_RNN), 0.01),
        'rg_wx': nrm(ks[18], (N_REC, RG_HEADS, RG_BW, RG_BW), RG_BW ** -0.5),
        'rg_bx': nrm(ks[19], (N_REC, D_RNN), 0.01),
        'rg_lambda': jnp.log(a0) - jnp.log1p(-a0),
        'dn_a_log': jnp.log(jax.random.uniform(ks[20], (N_REC, DN_HEADS), f32, 1.0, 16.0)),
        'dn_dt_bias': nrm(ks[21], (N_REC, DN_HEADS), 0.1),
        'dn_norm': 1.0 + nrm(ks[22], (N_REC, DN_DV), 0.01),
        'w_out_rec': nrm(ks[23], (N_REC, D_RNN + DN_VD, D_MODEL), (D_RNN + DN_VD) ** -0.5),
        'w_in_att': nrm(ks[24], (N_ATT, D_MODEL, ATT_IN), D_MODEL ** -0.5),
        'w_out_att': nrm(ks[25], (N_ATT, ATT_HEADS * HEAD_DIM, D_MODEL), (ATT_HEADS * HEAD_DIM) ** -0.5),
        'w_up': nrm(ks[26], (DEPTH, D_MODEL, D_FF), D_MODEL ** -0.5),
        'w_down': nrm(ks[27], (DEPTH, D_FF, D_MODEL), D_FF ** -0.5),
    }


def reference(x_prompt, x_sample, state_conv, state_h, state_S, cache_k, cache_v, cache_kidx, page_table,
              norm_mix, norm_mlp, norm_final, w_in_rec, w_conv, b_conv, rg_wa, rg_ba, rg_wx, rg_bx,
              rg_lambda, dn_a_log, dn_dt_bias, dn_norm, w_out_rec, w_in_att, w_out_att, w_up, w_down):
    B, S, _ = x_prompt.shape
    DB, T, _ = x_sample.shape
    past = page_table.shape[1] * cache_k.shape[2]
    first_p = jnp.arange(S) == 0
    first_s = (past + jnp.arange(T)) == 0
    dt = x_prompt.dtype
    xp, xs = x_prompt, x_sample
    conv_p, h_p, S_p, conv_s, h_s, S_s = [], [], [], [], [], []
    k_p, v_p, ki_p, k_s, v_s, ki_s = [], [], [], [], [], []
    for layer in range(DEPTH):
        hp = rmsnorm(xp, norm_mix[layer])
        hs = rmsnorm(xs, norm_mix[layer])
        if layer % 2 == 0:
            r = layer // 2
            prm = (w_in_rec[r], w_conv[r], b_conv[r], rg_wa[r], rg_ba[r], rg_wx[r], rg_bx[r], rg_lambda[r],
                   dn_a_log[r], dn_dt_bias[r], dn_norm[r], w_out_rec[r])
            yp, cb, hT, ST = rec_mixer(hp, jnp.zeros((B, CONV_W - 1, CONV_CH), dt), jnp.zeros((B, D_RNN), dt),
                                       jnp.zeros((B, DN_HEADS, DN_DK, DN_DV), dt), first_p, *prm)
            conv_p.append(cb); h_p.append(hT); S_p.append(ST)
            ys, cb, hT, ST = rec_mixer(hs, state_conv[r], state_h[r], state_S[r], first_s, *prm)
            conv_s.append(cb); h_s.append(hT); S_s.append(ST)
        else:
            a = layer // 2
            yp, kk, vv, kii = dsa_prompt(hp, w_in_att[a], w_out_att[a])
            k_p.append(kk); v_p.append(vv); ki_p.append(kii)
            ys, kk, vv, kii = dsa_sample(hs, cache_k[a], cache_v[a], cache_kidx[a], page_table, w_in_att[a], w_out_att[a])
            k_s.append(kk); v_s.append(vv); ki_s.append(kii)
        xp = xp + yp
        xs = xs + ys
        xp = xp + mlp(rmsnorm(xp, norm_mlp[layer]), w_up[layer], w_down[layer])
        xs = xs + mlp(rmsnorm(xs, norm_mlp[layer]), w_up[layer], w_down[layer])
    y_prompt = rmsnorm(xp, norm_final)
    y_sample = rmsnorm(xs, norm_final)
    return (y_prompt, y_sample,
            jnp.stack(conv_p), jnp.stack(h_p), jnp.stack(S_p),
            jnp.stack(k_p), jnp.stack(v_p), jnp.stack(ki_p),
            jnp.stack(conv_s), jnp.stack(h_s), jnp.stack(S_s),
            jnp.stack(k_s), jnp.stack(v_s), jnp.stack(ki_s))
```

```python
import functools

import jax
import jax.numpy as jnp
from jax import lax
from jax.experimental import pallas as pl
from jax.experimental.pallas import tpu as pltpu

F32 = jnp.float32
BF16 = jnp.bfloat16
I32 = jnp.int32

EPS = 1e-6
CONV_W = 4
D_RNN_HEADS = 8
RG_C = 8.0
DN_HEADS = 4
DN_D = 128
ATT_HEADS = 8
KV_HEADS = 2
GROUP = ATT_HEADS // KV_HEADS
HEAD_DIM = 128
IDX_HEADS = 8
IDX_DIM = 64
TOPK_MAX = 256
LANES = 128
SUBLANES = 8
VMEM_LIMIT = 48 * 1024 * 1024
INT_MIN = -2 ** 31
NEG_BIG = -1e30
LOG2E = 1.4426950408889634
CAST_TILE_ELEMS = 1 << 21


def _cparams(*sem):
    return pltpu.CompilerParams(dimension_semantics=sem, vmem_limit_bytes=VMEM_LIMIT)


def _sigmoid(x):
    return 1.0 / (1.0 + jnp.exp(-x))


def _softplus(x):
    return jnp.maximum(x, 0.0) + jnp.log(1.0 + jnp.exp(-jnp.abs(x)))


def _silu(x):
    return x * _sigmoid(x)


def _gelu_tanh(x):
    return 0.5 * x * (1.0 + jnp.tanh(0.7978845608028654 * (x + 0.044715 * x * x * x)))


def _rms(x, g):
    return x * lax.rsqrt(jnp.mean(x * x, axis=-1, keepdims=True) + EPS) * g


def _dot(a, b):
    return jnp.dot(a, b, preferred_element_type=F32)


def _dot_nt(a, b):
    return lax.dot_general(a, b, (((1,), (1,)), ((), ())), preferred_element_type=F32)


def _norm_matmul_kernel(x_ref, g_ref, w_ref, o_ref, *rest):
    xn = _rms(x_ref[...], g_ref[...]).astype(BF16)
    y = _dot(xn, w_ref[...])
    o_ref[...] = y
    if rest:
        rest[0][...] = y.astype(BF16)


def norm_matmul(x, g, w, *, tm, with_bf16=False):
    m, d = x.shape
    n = w.shape[1]
    out_shape = [jax.ShapeDtypeStruct((m, n), F32)]
    out_specs = [pl.BlockSpec((tm, n), lambda i: (i, 0))]
    if with_bf16:
        out_shape.append(jax.ShapeDtypeStruct((m, n), BF16))
        out_specs.append(pl.BlockSpec((tm, n), lambda i: (i, 0)))
    res = pl.pallas_call(
        _norm_matmul_kernel,
        grid=(m // tm,),
        in_specs=[pl.BlockSpec((tm, d), lambda i: (i, 0)),
                  pl.BlockSpec((1, d), lambda i: (0, 0)),
                  pl.BlockSpec((d, n), lambda i: (0, 0))],
        out_specs=out_specs,
        out_shape=out_shape,
        compiler_params=_cparams("parallel"),
        name="norm_matmul",
    )(x, g.reshape(1, d), w)
    return res if with_bf16 else res[0]


def _proj_residual_kernel(*refs, n_in):
    a_refs = refs[:n_in]
    w_refs = refs[n_in:2 * n_in]
    x_ref = refs[2 * n_in]
    o_ref = refs[2 * n_in + 1]
    acc = x_ref[...]
    for a_ref, w_ref in zip(a_refs, w_refs):
        acc = acc + _dot(a_ref[...].astype(BF16), w_ref[...])
    o_ref[...] = acc


def proj_residual(acts, ws, x, *, tm):
    m, d = x.shape
    n_in = len(acts)
    in_specs = [pl.BlockSpec((tm, a.shape[1]), lambda i: (i, 0)) for a in acts]
    in_specs += [pl.BlockSpec(w.shape, lambda i: (0, 0)) for w in ws]
    in_specs += [pl.BlockSpec((tm, d), lambda i: (i, 0))]
    return pl.pallas_call(
        functools.partial(_proj_residual_kernel, n_in=n_in),
        grid=(m // tm,),
        in_specs=in_specs,
        out_specs=pl.BlockSpec((tm, d), lambda i: (i, 0)),
        out_shape=jax.ShapeDtypeStruct((m, d), F32),
        compiler_params=_cparams("parallel"),
        name="proj_residual",
    )(*acts, *ws, x)


def _mlp_kernel(x_ref, g_ref, wu_ref, wd_ref, o_ref, xn_ref):
    f = pl.program_id(1)

    @pl.when(f == 0)
    def _():
        x = x_ref[...]
        xn_ref[...] = _rms(x, g_ref[...]).astype(BF16)
        o_ref[...] = x

    h = jnp.maximum(_dot(xn_ref[...], wu_ref[...]), 0.0)
    o_ref[...] += _dot((h * h).astype(BF16), wd_ref[...])


def mlp_residual(x, g, w_up, w_down, *, tm, tf):
    m, d = x.shape
    dff = w_up.shape[1]
    return pl.pallas_call(
        _mlp_kernel,
        grid=(m // tm, dff // tf),
        in_specs=[pl.BlockSpec((tm, d), lambda i, f: (i, 0)),
                  pl.BlockSpec((1, d), lambda i, f: (0, 0)),
                  pl.BlockSpec((d, tf), lambda i, f: (0, f)),
                  pl.BlockSpec((tf, d), lambda i, f: (f, 0))],
        out_specs=pl.BlockSpec((tm, d), lambda i, f: (i, 0)),
        out_shape=jax.ShapeDtypeStruct((m, d), F32),
        scratch_shapes=[pltpu.VMEM((tm, d), BF16)],
        compiler_params=_cparams("parallel", "arbitrary"),
        name="mlp_residual",
    )(x, g.reshape(1, d), w_up, w_down)


def _final_norm_kernel(x_ref, g_ref, o_ref):
    o_ref[...] = _rms(x_ref[...], g_ref[...])


def final_norm(x, g, *, tm):
    m, d = x.shape
    return pl.pallas_call(
        _final_norm_kernel,
        grid=(m // tm,),
        in_specs=[pl.BlockSpec((tm, d), lambda i: (i, 0)),
                  pl.BlockSpec((1, d), lambda i: (0, 0))],
        out_specs=pl.BlockSpec((tm, d), lambda i: (i, 0)),
        out_shape=jax.ShapeDtypeStruct((m, d), F32),
        compiler_params=_cparams("parallel"),
        name="final_norm",
    )(x, g.reshape(1, d))


def _cast_kernel(x_ref, o_ref):
    o_ref[...] = x_ref[...].astype(o_ref.dtype)


def cast_bf16(w):
    n = w.shape[-1]
    w2 = w.reshape(-1, n)
    m = w2.shape[0]
    tm = m
    while tm * n > CAST_TILE_ELEMS and tm % (4 * SUBLANES) == 0:
        tm //= 2
    out = pl.pallas_call(
        _cast_kernel,
        grid=(m // tm,),
        in_specs=[pl.BlockSpec((tm, n), lambda i: (i, 0))],
        out_specs=pl.BlockSpec((tm, n), lambda i: (i, 0)),
        out_shape=jax.ShapeDtypeStruct((m, n), BF16),
        compiler_params=_cparams("parallel"),
        name="cast_bf16",
    )(w2)
    return out.reshape(w.shape)


def _causal_conv(x, prev, wc):
    tt, c = x.shape
    row = lax.broadcasted_iota(I32, (SUBLANES, c), 0)
    acc = x * wc[CONV_W - 1:CONV_W]
    for k in range(1, CONV_W):
        r = pltpu.roll(x, k, 0)
        top = jnp.where(row < k, pltpu.roll(prev, k, 0), r[:SUBLANES])
        r = jnp.concatenate([top, r[SUBLANES:]], axis=0)
        acc = acc + r * wc[CONV_W - 1 - k:CONV_W - k]
    return acc


def _rglru_kernel(u_ref, gate_ref, wc_ref, bc_ref, wa_ref, ba_ref, wx_ref, bx_ref, lam_ref,
                  conv0_ref, h0_ref, y_ref, ht_ref, carry_ref, h_ref, *, tt, first_at_zero, t_valid):
    j = pl.program_id(1)

    @pl.when(j == 0)
    def _():
        carry_ref[...] = conv0_ref[0]
        h_ref[...] = h0_ref[0]

    x = u_ref[0]
    c = x.shape[1]
    xr = _causal_conv(x, carry_ref[...], wc_ref[...]) + bc_ref[...]
    carry_ref[...] = x[tt - SUBLANES:]
    xb = xr.astype(BF16)
    r_g = _sigmoid(_dot(xb, wa_ref[...]) + ba_ref[...])
    i_g = _sigmoid(_dot(xb, wx_ref[...]) + bx_ref[...])
    log_a = -RG_C * r_g * _softplus(-lam_ref[...])
    a = jnp.exp(log_a)
    mult = jnp.sqrt(jnp.maximum(1.0 - jnp.exp(2.0 * log_a), 0.0))
    row = lax.broadcasted_iota(I32, (tt, c), 0)
    if first_at_zero:
        mult = jnp.where(row + j * tt == 0, 1.0, mult)
    b = xr * i_g * mult
    k = 1
    while k < tt:
        keep = row >= k
        a_s = jnp.where(keep, pltpu.roll(a, k, 0), 1.0)
        b_s = jnp.where(keep, pltpu.roll(b, k, 0), 0.0)
        b = a * b_s + b
        a = a * a_s
        k *= 2
    h = a * h_ref[...] + b
    h_ref[...] = h[tt - 1:tt]
    y_ref[0] = (h * _gelu_tanh(gate_ref[0])).astype(y_ref.dtype)
    j_last, r_last = divmod(t_valid - 1, tt)

    @pl.when(j == j_last)
    def _():
        ht_ref[0] = h[r_last:r_last + 1]


def rglru(p3, wc, bc, wa_bd, ba, wx_bd, bx, lam, conv0, h0, *, tt, first_at_zero, t_valid):
    b, t, _ = p3.shape
    c = wc.shape[1]
    vec = lambda: pl.BlockSpec((1, c), lambda i, j: (0, 0))
    return pl.pallas_call(
        functools.partial(_rglru_kernel, tt=tt, first_at_zero=first_at_zero, t_valid=t_valid),
        grid=(b, t // tt),
        in_specs=[pl.BlockSpec((1, tt, c), lambda i, j: (i, j, 0)),
                  pl.BlockSpec((1, tt, c), lambda i, j: (i, j, 2048 // c)),
                  pl.BlockSpec((CONV_W, c), lambda i, j: (0, 0)),
                  vec(),
                  pl.BlockSpec((c, c), lambda i, j: (0, 0)),
                  vec(),
                  pl.BlockSpec((c, c), lambda i, j: (0, 0)),
                  vec(), vec(),
                  pl.BlockSpec((1, SUBLANES, c), lambda i, j: (i, 0, 0)),
                  pl.BlockSpec((1, 1, c), lambda i, j: (i, 0, 0))],
        out_specs=[pl.BlockSpec((1, tt, c), lambda i, j: (i, j, 0)),
                   pl.BlockSpec((1, 1, c), lambda i, j: (i, 0, 0))],
        out_shape=[jax.ShapeDtypeStruct((b, t, c), BF16),
                   jax.ShapeDtypeStruct((b, 1, c), F32)],
        scratch_shapes=[pltpu.VMEM((SUBLANES, c), F32), pltpu.VMEM((1, c), F32)],
        compiler_params=_cparams("parallel", "arbitrary"),
        name="rglru",
    )(p3, p3, wc, bc, wa_bd, ba, wx_bd, bx, lam, conv0, h0)


def _split3(x):
    hi = x.astype(BF16)
    r1 = x - hi.astype(F32)
    mid = r1.astype(BF16)
    lo = (r1 - mid.astype(F32)).astype(BF16)
    return hi, mid, lo


def _l2norm(x):
    return x * lax.rsqrt(jnp.sum(x * x, axis=-1, keepdims=True) + EPS)


def _dn_kernel(uq_ref, uk_ref, uv_ref, z_ref, ba_ref, wc_ref, par_ref, nw_ref, conv0_ref, s0_ref,
               y_ref, st_ref, carry_ref, s_ref, *, tt, t_valid, t_total):
    j = pl.program_id(1)
    nj = pl.num_programs(1)
    hd = DN_HEADS * DN_D

    @pl.when(j == 0)
    def _():
        carry_ref[...] = conv0_ref[0]
        s_ref[...] = s0_ref[0]

    wc = wc_ref[...]
    prev = carry_ref[...]
    parts = []
    for n, ref in enumerate((uq_ref, uk_ref, uv_ref)):
        x = ref[0]
        sl = slice(n * hd, (n + 1) * hd)
        parts.append(_silu(_causal_conv(x, prev[:, sl], wc[:, sl])))
        carry_ref[:, sl] = x[tt - SUBLANES:]
    q_all, k_all, v_all = parts

    ba = ba_ref[0]
    par = par_ref[...]
    beta_all = _sigmoid(ba)
    g_all = -jnp.exp(par[0:1]) * _softplus(ba + par[1:2])
    if t_valid < t_total:
        live = lax.broadcasted_iota(I32, (tt, LANES), 0) + j * tt < t_valid
        beta_all = jnp.where(live, beta_all, 0.0)
        g_all = jnp.where(live, g_all, 0.0)

    ri = lax.broadcasted_iota(I32, (tt, tt), 0)
    ci = lax.broadcasted_iota(I32, (tt, tt), 1)
    lower = ri >= ci
    strict = ri > ci
    tri = jnp.where(lower, 1.0, 0.0).astype(BF16)
    eye = jnp.where(ri == ci, 1.0, 0.0)
    gc_all = sum(_dot(tri, part) for part in _split3(g_all))
    gc_t = jnp.transpose(gc_all)
    gl_row = gc_all[tt - 1:tt]
    eg_all = jnp.exp(gc_all)
    egl_all = jnp.exp(gl_row - gc_all)
    glast_row = jnp.exp(gl_row)
    nw = nw_ref[...]

    for h in range(DN_HEADS):
        hs = slice(h * DN_D, (h + 1) * DN_D)
        q = _l2norm(q_all[:, hs]) * (DN_D ** -0.5)
        k = _l2norm(k_all[:, hs])
        v = v_all[:, hs]
        beta = beta_all[:, h:h + 1]
        gcol = 4 + h
        dmat = gc_all[:, gcol:gcol + 1] - gc_t[gcol:gcol + 1, :]
        decay = jnp.exp(jnp.where(lower, dmat, NEG_BIG))
        eg = eg_all[:, gcol:gcol + 1]
        egl = egl_all[:, gcol:gcol + 1]
        kb16 = k.astype(BF16)
        kk = _dot_nt(kb16, kb16)
        qk = _dot_nt(q.astype(BF16), kb16)
        x = jnp.where(strict, -(beta * kk * decay), 0.0)
        attn = qk * decay
        pmat = eye + x
        xp = x
        span = 2
        while span < tt:
            xp16 = xp.astype(BF16)
            xp = _dot(xp16, xp16)
            pmat = pmat + _dot(xp.astype(BF16), pmat.astype(BF16))
            span *= 2
        kbeta = k * beta
        rhs = jnp.concatenate([v * beta, kbeta * eg], axis=1).astype(BF16)
        sol = _dot(pmat.astype(BF16), rhs)
        u = sol[:, :DN_D]
        w = sol[:, DN_D:]
        qg = q * eg
        kd = k * egl
        s_old = s_ref[h]
        r = _dot(jnp.concatenate([w, qg], axis=0).astype(BF16), s_old.astype(BF16))
        vnew = u - r[:tt]
        vnew16 = vnew.astype(BF16)
        o = r[tt:] + _dot(attn.astype(BF16), vnew16)
        s_new = s_old * glast_row[:, gcol:gcol + 1] + _dot(jnp.transpose(kd).astype(BF16), vnew16)
        s_ref[h] = s_new
        y = _rms(o, nw) * _silu(z_ref[0][:, hs])
        y_ref[0, :, hs] = y.astype(y_ref.dtype)

    @pl.when(j == nj - 1)
    def _():
        st_ref[0] = s_ref[...]


def gated_delta(p3, wc, par, nw, conv0, s0, *, tt, t_valid):
    b, t, _ = p3.shape
    hd = DN_HEADS * DN_D
    return pl.pallas_call(
        functools.partial(_dn_kernel, tt=tt, t_valid=t_valid, t_total=t),
        grid=(b, t // tt),
        in_specs=[pl.BlockSpec((1, tt, hd), lambda i, j: (i, j, 1)),
                  pl.BlockSpec((1, tt, hd), lambda i, j: (i, j, 2)),
                  pl.BlockSpec((1, tt, hd), lambda i, j: (i, j, 3)),
                  pl.BlockSpec((1, tt, hd), lambda i, j: (i, j, 5)),
                  pl.BlockSpec((1, tt, LANES), lambda i, j: (i, j, 3072 // LANES)),
                  pl.BlockSpec((CONV_W, 3 * hd), lambda i, j: (0, 0)),
                  pl.BlockSpec((SUBLANES, LANES), lambda i, j: (0, 0)),
                  pl.BlockSpec((1, DN_D), lambda i, j: (0, 0)),
                  pl.BlockSpec((1, SUBLANES, 3 * hd), lambda i, j: (i, 0, 0)),
                  pl.BlockSpec((1, DN_HEADS, DN_D, DN_D), lambda i, j: (i, 0, 0, 0))],
        out_specs=[pl.BlockSpec((1, tt, hd), lambda i, j: (i, j, 0)),
                   pl.BlockSpec((1, DN_HEADS, DN_D, DN_D), lambda i, j: (i, 0, 0, 0))],
        out_shape=[jax.ShapeDtypeStruct((b, t, hd), BF16),
                   jax.ShapeDtypeStruct((b, DN_HEADS, DN_D, DN_D), F32)],
        scratch_shapes=[pltpu.VMEM((SUBLANES, 3 * hd), F32),
                        pltpu.VMEM((DN_HEADS, DN_D, DN_D), F32)],
        compiler_params=_cparams("parallel", "arbitrary"),
        name="gated_delta",
    )(p3, p3, p3, p3, p3, wc, par, nw, conv0, s0)


def _order_key(score):
    bits = pltpu.bitcast(score, I32)
    return bits ^ ((bits >> 31) & 0x7FFFFFFF)


def _select_topk(key_ref, n_slots, topk, pos_fn, partial_fn, final_fn, acc_shape, vec_shape, pos_bits):
    def count(pred):
        def body(c, acc):
            return acc + partial_fn(jnp.where(pred(key_ref[c], c), 1, 0))

        return final_fn(lax.fori_loop(0, n_slots, body, jnp.zeros(acc_shape, I32)))

    def bit_body(bi, lo):
        cand = lo + lax.shift_left(jnp.int32(1), 31 - bi)
        return jnp.where(count(lambda kk, c: kk >= cand) >= topk, cand, lo)

    lo = lax.fori_loop(0, 32, bit_body, jnp.full(vec_shape, INT_MIN, I32))
    thr = jnp.maximum(lo, INT_MIN + 1)
    tie = (lo > INT_MIN) & (count(lambda kk, c: kk >= thr) > topk)

    @pl.when(jnp.max(jnp.where(tie, 1.0, 0.0)) > 0.5)
    def _():
        need = topk - count(lambda kk, c: kk > thr)

        def pos_body(bi, j):
            cand = j + lax.shift_left(jnp.int32(1), pos_bits - 1 - bi)
            below = count(lambda kk, c: (kk == thr) & (pos_fn(c) < cand))
            return jnp.where(below < need, cand, j)

        jmax = lax.fori_loop(0, pos_bits, pos_body, jnp.zeros(vec_shape, I32))

        def fix(c, carry):
            kk = key_ref[c]
            drop = tie & (kk == thr) & (pos_fn(c) > jmax)
            key_ref[c] = jnp.where(drop, INT_MIN, kk)
            return carry

        lax.fori_loop(0, n_slots, fix, 0)

    return thr


def _dsa_prompt_kernel(q_ref, qi_ref, wi_ref, k_ref, v_ref, ki_ref, o_ref,
                       key_ref, vt_ref, qs_ref, qis_ref, s_ref, m_ref, l_ref, acc_ref, *, qb, kc, topk):
    i = pl.program_id(1)
    s_len = k_ref.shape[1]
    nc = (i * qb + qb + kc - 1) // kc

    @pl.when(i == 0)
    def _():
        for c in range(s_len // kc):
            vt_ref[c] = jnp.transpose(v_ref[0, c * kc:(c + 1) * kc, :].astype(F32)).astype(BF16)

    q = q_ref[0] * (HEAD_DIM ** -0.5 * LOG2E)
    qi = qi_ref[0]
    for g in range(KV_HEADS):
        for r in range(GROUP):
            hh = g * GROUP + r
            qs_ref[g, r * qb:(r + 1) * qb, :] = q[:, hh * HEAD_DIM:(hh + 1) * HEAD_DIM].astype(BF16)
    for h in range(IDX_HEADS):
        qis_ref[h * qb:(h + 1) * qb, :] = qi[:, h * LANES:h * LANES + IDX_DIM]
    w_t = jnp.transpose(wi_ref[0])
    tpos = i * qb + lax.broadcasted_iota(I32, (kc, qb), 1)
    krow = lax.broadcasted_iota(I32, (kc, qb), 0)

    def score_body(c, carry):
        start = pl.multiple_of(c * kc, kc)
        keys = ki_ref[0, pl.ds(start, kc), :][:, :IDX_DIM]
        s_all = _dot_nt(keys, qis_ref[...])
        sc = None
        for h in range(IDX_HEADS):
            term = w_t[h:h + 1, :] * jnp.maximum(s_all[:, h * qb:(h + 1) * qb], 0.0)
            sc = term if sc is None else sc + term
        key_ref[c] = jnp.where(krow + c * kc <= tpos, _order_key(sc), INT_MIN)
        return carry

    lax.fori_loop(0, nc, score_body, 0)

    thr = _select_topk(
        key_ref, nc, topk,
        pos_fn=lambda c: krow + c * kc,
        partial_fn=lambda x: jnp.sum(x.reshape(kc // SUBLANES, SUBLANES, qb), axis=0),
        final_fn=lambda a: jnp.sum(a, axis=0, keepdims=True),
        acc_shape=(SUBLANES, qb), vec_shape=(1, qb), pos_bits=(s_len - 1).bit_length())

    m_ref[...] = jnp.full(m_ref.shape, NEG_BIG, F32)
    l_ref[...] = jnp.zeros(l_ref.shape, F32)
    acc_ref[...] = jnp.zeros(acc_ref.shape, F32)

    def logits(c, g):
        start = pl.multiple_of(c * kc, kc)
        return _dot_nt(k_ref[0, pl.ds(start, kc), g * HEAD_DIM:(g + 1) * HEAD_DIM], qs_ref[g])

    def accumulate(g, s, bias_g, vt_g):
        s = s + bias_g
        m_old = m_ref[g]
        m_new = jnp.maximum(m_old, jnp.max(s, axis=0, keepdims=True))
        p = jnp.exp2(s - m_new)
        alpha = jnp.exp2(m_old - m_new)
        l_ref[g] = alpha * l_ref[g] + jnp.sum(p, axis=0, keepdims=True)
        acc_ref[g] = alpha * acc_ref[g] + _dot(vt_g, p.astype(BF16))
        m_ref[g] = m_new

    s_ref[...] = logits(0, 0)

    def attn_body(c, carry):
        s1 = logits(c, 1)
        bias = jnp.where(key_ref[c] >= thr, 0.0, NEG_BIG)
        bias_g = jnp.concatenate([bias] * GROUP, axis=1)
        vtc = vt_ref[c]
        accumulate(0, s_ref[...], bias_g, vtc[:HEAD_DIM])
        s_ref[...] = logits(jnp.minimum(c + 1, nc - 1), 0)
        accumulate(1, s1, bias_g, vtc[HEAD_DIM:])
        return carry

    lax.fori_loop(0, nc, attn_body, 0)
    for g in range(KV_HEADS):
        out_t = acc_ref[g] / l_ref[g]
        for r in range(GROUP):
            hh = g * GROUP + r
            o_ref[0, :, hh * HEAD_DIM:(hh + 1) * HEAD_DIM] = jnp.transpose(
                out_t[:, r * qb:(r + 1) * qb]).astype(o_ref.dtype)


ATT_Q0 = 0
ATT_QI0 = 1024
ATT_K0 = 2048
ATT_V0 = 2304
ATT_KI0 = 2560
ATT_WI0 = 2688
ATT_NP = 2816


def dsa_prompt(p3, pb3, *, qb, kc):
    b, s, _ = p3.shape
    topk = min(TOPK_MAX, s // 4)
    hw = ATT_HEADS * HEAD_DIM
    kvw = KV_HEADS * HEAD_DIM
    return pl.pallas_call(
        functools.partial(_dsa_prompt_kernel, qb=qb, kc=kc, topk=topk),
        grid=(b, s // qb),
        in_specs=[pl.BlockSpec((1, qb, hw), lambda bi, i: (bi, i, ATT_Q0 // hw)),
                  pl.BlockSpec((1, qb, IDX_HEADS * LANES), lambda bi, i: (bi, i, ATT_QI0 // (IDX_HEADS * LANES))),
                  pl.BlockSpec((1, qb, LANES), lambda bi, i: (bi, i, ATT_WI0 // LANES)),
                  pl.BlockSpec((1, s, kvw), lambda bi, i: (bi, 0, ATT_K0 // kvw)),
                  pl.BlockSpec((1, s, kvw), lambda bi, i: (bi, 0, ATT_V0 // kvw)),
                  pl.BlockSpec((1, s, LANES), lambda bi, i: (bi, 0, ATT_KI0 // LANES))],
        out_specs=pl.BlockSpec((1, qb, hw), lambda bi, i: (bi, i, 0)),
        out_shape=jax.ShapeDtypeStruct((b, s, hw), BF16),
        scratch_shapes=[pltpu.VMEM((s // kc, kc, qb), I32),
                        pltpu.VMEM((s // kc, kvw, kc), BF16),
                        pltpu.VMEM((KV_HEADS, GROUP * qb, HEAD_DIM), BF16),
                        pltpu.VMEM((IDX_HEADS * qb, IDX_DIM), BF16),
                        pltpu.VMEM((kc, GROUP * qb), F32),
                        pltpu.VMEM((KV_HEADS, 1, GROUP * qb), F32),
                        pltpu.VMEM((KV_HEADS, 1, GROUP * qb), F32),
                        pltpu.VMEM((KV_HEADS, HEAD_DIM, GROUP * qb), F32)],
        compiler_params=_cparams("parallel", "arbitrary"),
        name="dsa_prompt",
    )(p3, pb3, p3, pb3, pb3, pb3)


def _index_scores(qi, wi, keys, keys_transposed):
    sc = None
    for h in range(IDX_HEADS):
        qh = qi[:, h * LANES:h * LANES + IDX_DIM]
        s = _dot(qh, keys) if keys_transposed else _dot_nt(qh, keys)
        term = wi[:, h:h + 1] * jnp.maximum(s, 0.0)
        sc = term if sc is None else sc + term
    return sc


def _dsa_sample_select_kernel(pt_ref, qi_ref, wi_ref, kin_ref, *rest, pages, t_new, topk):
    page_refs = rest[:pages]
    bias_ref = rest[pages]
    key_ref = rest[pages + 1]
    j = pl.program_id(1)
    nj = pl.num_programs(1)
    rows = qi_ref.shape[1]
    width = pages * LANES
    qi = qi_ref[0].astype(BF16)
    wi = wi_ref[0]
    keys_t = jnp.concatenate([r[...] for r in page_refs], axis=1).astype(BF16)
    key_ref[j] = _order_key(_index_scores(qi, wi, keys_t, True))

    @pl.when(j == nj - 1)
    def _():
        kin = kin_ref[0][:, :IDX_DIM].astype(BF16)
        kin = jnp.concatenate([kin, jnp.zeros((LANES - rows, IDX_DIM), BF16)], axis=0)
        knew = _order_key(_index_scores(qi, wi, kin, False))
        ri = lax.broadcasted_iota(I32, (rows, LANES), 0)
        ci = lax.broadcasted_iota(I32, (rows, LANES), 1)
        knew = jnp.where((ci <= ri) & (ci < t_new), knew, INT_MIN)
        key_ref[nj] = jnp.concatenate([knew, jnp.full((rows, width - LANES), INT_MIN, I32)], axis=1)
        lane = lax.broadcasted_iota(I32, (rows, width), 1)
        thr = _select_topk(
            key_ref, nj + 1, topk,
            pos_fn=lambda c: lane + c * width,
            partial_fn=lambda x: sum(x[:, s * LANES:(s + 1) * LANES] for s in range(pages)),
            final_fn=lambda a: jnp.sum(a, axis=1, keepdims=True),
            acc_shape=(rows, LANES), vec_shape=(rows, 1), pos_bits=((nj + 1) * width - 1).bit_length())

        def out_body(c, carry):
            bias_ref[0, c] = jnp.where(key_ref[c] >= thr, 0.0, NEG_BIG)
            return carry

        lax.fori_loop(0, nj + 1, out_body, 0)


def dsa_sample_select(ps8, cache_kidx_t, layer, page_table, *, pages, t_new):
    db, rows, _ = ps8.shape
    n_pages = page_table.shape[1]
    page = cache_kidx_t.shape[3]
    nj = n_pages // pages
    width = pages * LANES
    topk = min(TOPK_MAX, (n_pages * page + t_new) // 4)

    def page_spec(r):
        return pl.BlockSpec((None, None, IDX_DIM, page),
                            lambda bi, j, pt: (layer, pt[bi * n_pages + j * pages + r], 0, 0))

    grid_spec = pltpu.PrefetchScalarGridSpec(
        num_scalar_prefetch=1,
        grid=(db, nj),
        in_specs=[pl.BlockSpec((1, rows, IDX_HEADS * LANES), lambda bi, j, pt: (bi, 0, ATT_QI0 // (IDX_HEADS * LANES))),
                  pl.BlockSpec((1, rows, LANES), lambda bi, j, pt: (bi, 0, ATT_WI0 // LANES)),
                  pl.BlockSpec((1, rows, LANES), lambda bi, j, pt: (bi, 0, ATT_KI0 // LANES))]
        + [page_spec(r) for r in range(pages)],
        out_specs=pl.BlockSpec((1, nj + 1, rows, width), lambda bi, j, pt: (bi, 0, 0, 0)),
        scratch_shapes=[pltpu.VMEM((nj + 1, rows, width), I32)],
    )
    return pl.pallas_call(
        functools.partial(_dsa_sample_select_kernel, pages=pages, t_new=t_new, topk=topk),
        grid_spec=grid_spec,
        out_shape=jax.ShapeDtypeStruct((db, nj + 1, rows, width), F32),
        compiler_params=_cparams("parallel", "arbitrary"),
        name="dsa_sample_select",
    )(page_table.reshape(-1), ps8, ps8, ps8, *([cache_kidx_t] * pages))


def _softmax_update(s, m_old, l_old, acc_old, vals):
    m_new = jnp.maximum(m_old, jnp.max(s, axis=1, keepdims=True))
    p = jnp.exp(s - m_new)
    alpha = jnp.exp(m_old - m_new)
    l_new = alpha * l_old + jnp.sum(p, axis=1, keepdims=True)
    acc_new = alpha * acc_old + _dot(p.astype(BF16), vals)
    return m_new, l_new, acc_new


def _dsa_sample_attend_kernel(pt_ref, q_ref, kn_ref, vn_ref, bias_ref, *rest, pages):
    k_refs = rest[:pages]
    v_refs = rest[pages:2 * pages]
    o_ref, qs_ref, m_ref, l_ref, acc_ref = rest[2 * pages:]
    j = pl.program_id(1)
    nj = pl.num_programs(1)
    rows = q_ref.shape[1]

    @pl.when(j == 0)
    def _():
        q = q_ref[0] * (HEAD_DIM ** -0.5)
        for g in range(KV_HEADS):
            for r in range(GROUP):
                hh = g * GROUP + r
                qs_ref[g, r * rows:(r + 1) * rows, :] = q[:, hh * HEAD_DIM:(hh + 1) * HEAD_DIM].astype(BF16)
        m_ref[...] = jnp.full(m_ref.shape, NEG_BIG, F32)
        l_ref[...] = jnp.zeros(l_ref.shape, F32)
        acc_ref[...] = jnp.zeros(acc_ref.shape, F32)

    def update(g, kch, vch, bias4):
        s = _dot_nt(qs_ref[g], kch) + bias4
        m_new, l_new, acc_new = _softmax_update(s, m_ref[g], l_ref[g], acc_ref[g], vch)
        m_ref[g] = m_new
        l_ref[g] = l_new
        acc_ref[g] = acc_new

    bias4 = jnp.concatenate([bias_ref[0, j]] * GROUP, axis=0)
    for g in range(KV_HEADS):
        kch = jnp.concatenate([r[:, g, :] for r in k_refs], axis=0).astype(BF16)
        vch = jnp.concatenate([r[:, g, :] for r in v_refs], axis=0).astype(BF16)
        update(g, kch, vch, bias4)

    @pl.when(j == nj - 1)
    def _():
        pad = jnp.zeros((LANES - rows, HEAD_DIM), BF16)
        bias_new = jnp.concatenate([bias_ref[0, nj][:, :LANES]] * GROUP, axis=0)
        for g in range(KV_HEADS):
            gs = slice(g * HEAD_DIM, (g + 1) * HEAD_DIM)
            kn = jnp.concatenate([kn_ref[0][:, gs].astype(BF16), pad], axis=0)
            vn = jnp.concatenate([vn_ref[0][:, gs].astype(BF16), pad], axis=0)
            update(g, kn, vn, bias_new)
            out = acc_ref[g] / l_ref[g]
            for r in range(GROUP):
                hh = g * GROUP + r
                o_ref[0, :, hh * HEAD_DIM:(hh + 1) * HEAD_DIM] = out[r * rows:(r + 1) * rows].astype(o_ref.dtype)


def dsa_sample_attend(ps8, bias, cache_k, cache_v, layer, page_table, *, pages):
    db, rows, _ = ps8.shape
    n_pages = page_table.shape[1]
    page = cache_k.shape[2]
    nj = n_pages // pages
    width = pages * LANES
    hw = ATT_HEADS * HEAD_DIM
    kvw = KV_HEADS * HEAD_DIM

    def page_spec(r):
        return pl.BlockSpec((None, None, page, KV_HEADS, HEAD_DIM),
                            lambda bi, j, pt: (layer, pt[bi * n_pages + j * pages + r], 0, 0, 0))

    grid_spec = pltpu.PrefetchScalarGridSpec(
        num_scalar_prefetch=1,
        grid=(db, nj),
        in_specs=[pl.BlockSpec((1, rows, hw), lambda bi, j, pt: (bi, 0, ATT_Q0 // hw)),
                  pl.BlockSpec((1, rows, kvw), lambda bi, j, pt: (bi, 0, ATT_K0 // kvw)),
                  pl.BlockSpec((1, rows, kvw), lambda bi, j, pt: (bi, 0, ATT_V0 // kvw)),
                  pl.BlockSpec((1, nj + 1, rows, width), lambda bi, j, pt: (bi, 0, 0, 0))]
        + [page_spec(r) for r in range(pages)] * 2,
        out_specs=pl.BlockSpec((1, rows, hw), lambda bi, j, pt: (bi, 0, 0)),
        scratch_shapes=[pltpu.VMEM((KV_HEADS, GROUP * rows, HEAD_DIM), BF16),
                        pltpu.VMEM((KV_HEADS, GROUP * rows, 1), F32),
                        pltpu.VMEM((KV_HEADS, GROUP * rows, 1), F32),
                        pltpu.VMEM((KV_HEADS, GROUP * rows, HEAD_DIM), F32)],
    )
    return pl.pallas_call(
        functools.partial(_dsa_sample_attend_kernel, pages=pages),
        grid_spec=grid_spec,
        out_shape=jax.ShapeDtypeStruct((db, rows, hw), BF16),
        compiler_params=_cparams("parallel", "arbitrary"),
        name="dsa_sample_attend",
    )(page_table.reshape(-1), ps8, ps8, ps8, bias, *([cache_k] * pages), *([cache_v] * pages))


REC_NP = 3200
SAMPLE_PAD_T = 128
SAMPLE_ROWS = 8
PAGES_PER_STEP = 8


def _block_diag(w):
    h, a, b = w.shape
    return jnp.einsum("hab,hg->hagb", w, jnp.eye(h, dtype=w.dtype)).reshape(h * a, h * b)


def _rec_weights(w_in16, w_conv, b_conv, wa16, ba, wx16, bx, lam, a_log, dt_bias, dn_w, w_out16):
    c = ba.shape[0]
    par = jnp.zeros((SUBLANES, LANES), F32)
    par = par.at[0, DN_HEADS:2 * DN_HEADS].set(a_log).at[1, DN_HEADS:2 * DN_HEADS].set(dt_bias)
    return dict(
        w_in=jnp.pad(w_in16, ((0, 0), (0, REC_NP - w_in16.shape[1]))),
        wc_rg=w_conv[:, :c], wc_dn=w_conv[:, c:], bc=b_conv.reshape(1, c),
        wa=_block_diag(wa16), ba=ba.reshape(1, c), wx=_block_diag(wx16),
        bx=bx.reshape(1, c), lam=lam.reshape(1, c), par=par, nw=dn_w.reshape(1, -1),
        w_out_rg=w_out16[:c], w_out_dn=w_out16[c:])


def _att_weights(w_in16, w_out16):
    d = w_in16.shape[0]
    hw = ATT_HEADS * HEAD_DIM
    kvw = KV_HEADS * HEAD_DIM
    o1, o3 = hw, hw + 2 * kvw
    o4 = o3 + IDX_HEADS * IDX_DIM
    o5 = o4 + IDX_DIM
    qi = w_in16[:, o3:o4].reshape(d, IDX_HEADS, IDX_DIM)
    qi = jnp.pad(qi, ((0, 0), (0, 0), (0, LANES - IDX_DIM))).reshape(d, IDX_HEADS * LANES)
    ki = jnp.pad(w_in16[:, o4:o5], ((0, 0), (0, LANES - IDX_DIM)))
    wi = jnp.pad(w_in16[:, o5:], ((0, 0), (0, LANES - IDX_HEADS)))
    return dict(w_in=jnp.concatenate([w_in16[:, :o1], qi, w_in16[:, o1:o3], ki, wi], axis=1), w_out=w_out16)


def _rec_layer(x2d, b, t, wts, g, conv_state, h_state, s_state, *, tm, tt, first_at_zero, t_valid):
    c = wts["bc"].shape[1]
    p = norm_matmul(x2d, g, wts["w_in"], tm=tm)
    p3 = p.reshape(b, t_valid, REC_NP)
    conv_new = p3[:, t_valid - (CONV_W - 1):, :conv_state.shape[-1]]
    if t != t_valid:
        p3 = jnp.pad(p3, ((0, 0), (0, t - t_valid), (0, 0)))
    conv0 = jnp.pad(conv_state, ((0, 0), (SUBLANES - (CONV_W - 1), 0), (0, 0)))
    rg_y, h_t = rglru(p3, wts["wc_rg"], wts["bc"], wts["wa"], wts["ba"], wts["wx"], wts["bx"], wts["lam"],
                      conv0[:, :, :c], h_state.reshape(b, 1, c), tt=tt, first_at_zero=first_at_zero,
                      t_valid=t_valid)
    dn_y, s_t = gated_delta(p3, wts["wc_dn"], wts["par"], wts["nw"], conv0[:, :, c:], s_state,
                            tt=tt, t_valid=t_valid)
    if t != t_valid:
        rg_y = rg_y[:, :t_valid]
        dn_y = dn_y[:, :t_valid]
    x2d = proj_residual([rg_y.reshape(b * t_valid, -1), dn_y.reshape(b * t_valid, -1)],
                        [wts["w_out_rg"], wts["w_out_dn"]], x2d, tm=tm)
    return x2d, conv_new, h_t.reshape(b, c), s_t


def _split_kv(p3):
    b, t, _ = p3.shape
    kvw = KV_HEADS * HEAD_DIM
    k = p3[:, :, ATT_K0:ATT_K0 + kvw].reshape(b, t, KV_HEADS, HEAD_DIM)
    v = p3[:, :, ATT_V0:ATT_V0 + kvw].reshape(b, t, KV_HEADS, HEAD_DIM)
    ki = p3[:, :, ATT_KI0:ATT_KI0 + IDX_DIM]
    return k, v, ki


def kernel(x_prompt, x_sample, state_conv, state_h, state_S, cache_k, cache_v, cache_kidx, page_table,
           norm_mix, norm_mlp, norm_final, w_in_rec, w_conv, b_conv, rg_wa, rg_ba, rg_wx, rg_bx,
           rg_lambda, dn_a_log, dn_dt_bias, dn_norm, w_out_rec, w_in_att, w_out_att, w_up, w_down):
    bsz, seq, d = x_prompt.shape
    db, t_new, _ = x_sample.shape
    depth = norm_mix.shape[0]
    past = page_table.shape[1] * cache_k.shape[2]
    xp = x_prompt.reshape(bsz * seq, d)
    xs = x_sample.reshape(db * t_new, d)
    tm_p = 256
    tm_s = db * t_new
    cache_kidx_t = jnp.swapaxes(cache_kidx, 2, 3)
    w_up16 = cast_bf16(w_up)
    w_down16 = cast_bf16(w_down)
    w_in_rec16 = cast_bf16(w_in_rec)
    w_out_rec16 = cast_bf16(w_out_rec)
    rg_wa16 = cast_bf16(rg_wa)
    rg_wx16 = cast_bf16(rg_wx)
    w_in_att16 = cast_bf16(w_in_att)
    w_out_att16 = cast_bf16(w_out_att)

    conv_p, h_p, s_p, conv_s, h_s, s_s = [], [], [], [], [], []
    k_p, v_p, ki_p, k_s, v_s, ki_s = [], [], [], [], [], []
    for layer in range(depth):
        if layer % 2 == 0:
            r = layer // 2
            wts = _rec_weights(w_in_rec16[r], w_conv[r], b_conv[r], rg_wa16[r], rg_ba[r], rg_wx16[r], rg_bx[r],
                               rg_lambda[r], dn_a_log[r], dn_dt_bias[r], dn_norm[r], w_out_rec16[r])
            conv_ch = state_conv.shape[-1]
            xp, cb, ht, st = _rec_layer(
                xp, bsz, seq, wts, norm_mix[layer], jnp.zeros((bsz, CONV_W - 1, conv_ch), F32),
                jnp.zeros((bsz, rg_ba.shape[1]), F32), jnp.zeros((bsz,) + state_S.shape[2:], F32),
                tm=tm_p, tt=256, first_at_zero=True, t_valid=seq)
            conv_p.append(cb); h_p.append(ht); s_p.append(st)
            xs, cb, ht, st = _rec_layer(
                xs, db, SAMPLE_PAD_T, wts, norm_mix[layer], state_conv[r], state_h[r], state_S[r],
                tm=tm_s, tt=SAMPLE_PAD_T, first_at_zero=(past == 0), t_valid=t_new)
            conv_s.append(cb); h_s.append(ht); s_s.append(st)
        else:
            a = layer // 2
            wts = _att_weights(w_in_att16[a], w_out_att16[a])
            p, pb = norm_matmul(xp, norm_mix[layer], wts["w_in"], tm=tm_p, with_bf16=True)
            p3 = p.reshape(bsz, seq, ATT_NP)
            o = dsa_prompt(p3, pb.reshape(bsz, seq, ATT_NP), qb=128, kc=512)
            xp = proj_residual([o.reshape(bsz * seq, -1)], [wts["w_out"]], xp, tm=tm_p)
            kk, vv, kii = _split_kv(p3)
            k_p.append(kk); v_p.append(vv); ki_p.append(kii)

            ps = norm_matmul(xs, norm_mix[layer], wts["w_in"], tm=tm_s)
            ps3 = ps.reshape(db, t_new, ATT_NP)
            ps8 = jnp.pad(ps3, ((0, 0), (0, SAMPLE_ROWS - t_new), (0, 0)))
            bias = dsa_sample_select(ps8, cache_kidx_t, a, page_table, pages=PAGES_PER_STEP, t_new=t_new)
            o = dsa_sample_attend(ps8, bias, cache_k, cache_v, a, page_table, pages=PAGES_PER_STEP)
            xs = proj_residual([o[:, :t_new].reshape(db * t_new, -1)], [wts["w_out"]], xs, tm=tm_s)
            kk, vv, kii = _split_kv(ps3)
            k_s.append(kk); v_s.append(vv); ki_s.append(kii)
        xp = mlp_residual(xp, norm_mlp[layer], w_up16[layer], w_down16[layer], tm=512, tf=1024)
        xs = mlp_residual(xs, norm_mlp[layer], w_up16[layer], w_down16[layer], tm=tm_s, tf=1024)
    y_prompt = final_norm(xp, norm_final, tm=512).reshape(bsz, seq, d)
    y_sample = final_norm(xs, norm_final, tm=tm_s).reshape(db, t_new, d)
    return (y_prompt, y_sample,
            jnp.stack(conv_p), jnp.stack(h_p), jnp.stack(s_p),
            jnp.stack(k_p), jnp.stack(v_p), jnp.stack(ki_p),
            jnp.stack(conv_s), jnp.stack(h_s), jnp.stack(s_s),
            jnp.stack(k_s), jnp.stack(v_s), jnp.stack(ki_s))
```

```python
import functools

import jax
import jax.numpy as jnp
from jax import lax
from jax.experimental import pallas as pl
from jax.experimental.pallas import tpu as pltpu

F32 = jnp.float32
BF16 = jnp.bfloat16
I32 = jnp.int32

EPS = 1e-6
CONV_W = 4
D_RNN_HEADS = 8
RG_C = 8.0
DN_HEADS = 4
DN_D = 128
ATT_HEADS = 8
KV_HEADS = 2
GROUP = ATT_HEADS // KV_HEADS
HEAD_DIM = 128
IDX_HEADS = 8
IDX_DIM = 64
TOPK_MAX = 256
LANES = 128
SUBLANES = 8
VMEM_LIMIT = 48 * 1024 * 1024
INT_MIN = -2 ** 31
NEG_BIG = -1e30
LOG2E = 1.4426950408889634
CAST_TILE_ELEMS = 1 << 21


def _cparams(*sem):
    return pltpu.CompilerParams(dimension_semantics=sem, vmem_limit_bytes=VMEM_LIMIT)


def _sigmoid(x):
    return 1.0 / (1.0 + jnp.exp(-x))


def _softplus(x):
    return jnp.maximum(x, 0.0) + jnp.log(1.0 + jnp.exp(-jnp.abs(x)))


def _silu(x):
    return x * _sigmoid(x)


def _gelu_tanh(x):
    return 0.5 * x * (1.0 + jnp.tanh(0.7978845608028654 * (x + 0.044715 * x * x * x)))


def _rms(x, g):
    return x * lax.rsqrt(jnp.mean(x * x, axis=-1, keepdims=True) + EPS) * g


def _dot(a, b):
    return jnp.dot(a, b, preferred_element_type=F32)


def _dot_nt(a, b):
    return lax.dot_general(a, b, (((1,), (1,)), ((), ())), preferred_element_type=F32)


def _norm_matmul_kernel(x_ref, g_ref, w_ref, o_ref, *rest):
    xn = _rms(x_ref[...], g_ref[...]).astype(BF16)
    y = _dot(xn, w_ref[...])
    o_ref[...] = y
    if rest:
        rest[0][...] = y.astype(BF16)


def norm_matmul(x, g, w, *, tm, with_bf16=False):
    m, d = x.shape
    n = w.shape[1]
    out_shape = [jax.ShapeDtypeStruct((m, n), F32)]
    out_specs = [pl.BlockSpec((tm, n), lambda i: (i, 0))]
    if with_bf16:
        out_shape.append(jax.ShapeDtypeStruct((m, n), BF16))
        out_specs.append(pl.BlockSpec((tm, n), lambda i: (i, 0)))
    res = pl.pallas_call(
        _norm_matmul_kernel,
        grid=(m // tm,),
        in_specs=[pl.BlockSpec((tm, d), lambda i: (i, 0)),
                  pl.BlockSpec((1, d), lambda i: (0, 0)),
                  pl.BlockSpec((d, n), lambda i: (0, 0))],
        out_specs=out_specs,
        out_shape=out_shape,
        compiler_params=_cparams("parallel"),
        name="norm_matmul",
    )(x, g.reshape(1, d), w)
    return res if with_bf16 else res[0]


def _proj_mlp_kernel(*refs, n_in):
    a_refs = refs[:n_in]
    w_refs = refs[n_in:2 * n_in]
    x_ref, g_ref, wu_ref, wd_ref, o_ref, xn_ref = refs[2 * n_in:]
    f = pl.program_id(1)

    @pl.when(f == 0)
    def _():
        x = x_ref[...]
        for a_ref, w_ref in zip(a_refs, w_refs):
            x = x + _dot(a_ref[...].astype(BF16), w_ref[...])
        xn_ref[...] = _rms(x, g_ref[...]).astype(BF16)
        o_ref[...] = x

    h = jnp.maximum(_dot(xn_ref[...], wu_ref[...]), 0.0)
    o_ref[...] += _dot((h * h).astype(BF16), wd_ref[...])


def proj_mlp(acts, ws, x, g, w_up, w_down, *, tm, tf):
    m, d = x.shape
    dff = w_up.shape[1]
    n_in = len(acts)
    in_specs = [pl.BlockSpec((tm, a.shape[1]), lambda i, f: (i, 0)) for a in acts]
    in_specs += [pl.BlockSpec(w.shape, lambda i, f: (0, 0)) for w in ws]
    in_specs += [pl.BlockSpec((tm, d), lambda i, f: (i, 0)),
                 pl.BlockSpec((1, d), lambda i, f: (0, 0)),
                 pl.BlockSpec((d, tf), lambda i, f: (0, f)),
                 pl.BlockSpec((tf, d), lambda i, f: (f, 0))]
    return pl.pallas_call(
        functools.partial(_proj_mlp_kernel, n_in=n_in),
        grid=(m // tm, dff // tf),
        in_specs=in_specs,
        out_specs=pl.BlockSpec((tm, d), lambda i, f: (i, 0)),
        out_shape=jax.ShapeDtypeStruct((m, d), F32),
        scratch_shapes=[pltpu.VMEM((tm, d), BF16)],
        compiler_params=_cparams("parallel", "arbitrary"),
        name="proj_mlp",
    )(*acts, *ws, x, g.reshape(1, d), w_up, w_down)


def _final_norm_kernel(x_ref, g_ref, o_ref):
    o_ref[...] = _rms(x_ref[...], g_ref[...])


def final_norm(x, g, *, tm):
    m, d = x.shape
    return pl.pallas_call(
        _final_norm_kernel,
        grid=(m // tm,),
        in_specs=[pl.BlockSpec((tm, d), lambda i: (i, 0)),
                  pl.BlockSpec((1, d), lambda i: (0, 0))],
        out_specs=pl.BlockSpec((tm, d), lambda i: (i, 0)),
        out_shape=jax.ShapeDtypeStruct((m, d), F32),
        compiler_params=_cparams("parallel"),
        name="final_norm",
    )(x, g.reshape(1, d))


def _cast_kernel(x_ref, o_ref):
    o_ref[...] = x_ref[...].astype(o_ref.dtype)


def cast_bf16(w):
    n = w.shape[-1]
    w2 = w.reshape(-1, n)
    m = w2.shape[0]
    tm = m
    while tm * n > CAST_TILE_ELEMS and tm % (4 * SUBLANES) == 0:
        tm //= 2
    out = pl.pallas_call(
        _cast_kernel,
        grid=(m // tm,),
        in_specs=[pl.BlockSpec((tm, n), lambda i: (i, 0))],
        out_specs=pl.BlockSpec((tm, n), lambda i: (i, 0)),
        out_shape=jax.ShapeDtypeStruct((m, n), BF16),
        compiler_params=_cparams("parallel"),
        name="cast_bf16",
    )(w2)
    return out.reshape(w.shape)


def _causal_conv(x, prev, wc):
    tt, c = x.shape
    row = lax.broadcasted_iota(I32, (SUBLANES, c), 0)
    acc = x * wc[CONV_W - 1:CONV_W]
    for k in range(1, CONV_W):
        r = pltpu.roll(x, k, 0)
        top = jnp.where(row < k, pltpu.roll(prev, k, 0), r[:SUBLANES])
        r = jnp.concatenate([top, r[SUBLANES:]], axis=0)
        acc = acc + r * wc[CONV_W - 1 - k:CONV_W - k]
    return acc


def _rglru_kernel(u_ref, gate_ref, wc_ref, bc_ref, wa_ref, ba_ref, wx_ref, bx_ref, lam_ref,
                  conv0_ref, h0_ref, y_ref, ht_ref, carry_ref, h_ref, *, tt, first_at_zero, t_valid):
    j = pl.program_id(1)

    @pl.when(j == 0)
    def _():
        carry_ref[...] = conv0_ref[0]
        h_ref[...] = h0_ref[0]

    x = u_ref[0]
    c = x.shape[1]
    xr = _causal_conv(x, carry_ref[...], wc_ref[...]) + bc_ref[...]
    carry_ref[...] = x[tt - SUBLANES:]
    xb = xr.astype(BF16)
    r_g = _sigmoid(_dot(xb, wa_ref[...]) + ba_ref[...])
    i_g = _sigmoid(_dot(xb, wx_ref[...]) + bx_ref[...])
    log_a = -RG_C * r_g * _softplus(-lam_ref[...])
    a = jnp.exp(log_a)
    mult = jnp.sqrt(jnp.maximum(1.0 - jnp.exp(2.0 * log_a), 0.0))
    row = lax.broadcasted_iota(I32, (tt, c), 0)
    if first_at_zero:
        mult = jnp.where(row + j * tt == 0, 1.0, mult)
    b = xr * i_g * mult
    k = 1
    while k < tt:
        keep = row >= k
        a_s = jnp.where(keep, pltpu.roll(a, k, 0), 1.0)
        b_s = jnp.where(keep, pltpu.roll(b, k, 0), 0.0)
        b = a * b_s + b
        a = a * a_s
        k *= 2
    h = a * h_ref[...] + b
    h_ref[...] = h[tt - 1:tt]
    y_ref[0] = (h * _gelu_tanh(gate_ref[0])).astype(y_ref.dtype)
    j_last, r_last = divmod(t_valid - 1, tt)

    @pl.when(j == j_last)
    def _():
        ht_ref[0] = h[r_last:r_last + 1]


def rglru(p3, wc, bc, wa_bd, ba, wx_bd, bx, lam, conv0, h0, *, tt, first_at_zero, t_valid):
    b, t, _ = p3.shape
    c = wc.shape[1]
    vec = lambda: pl.BlockSpec((1, c), lambda i, j: (0, 0))
    return pl.pallas_call(
        functools.partial(_rglru_kernel, tt=tt, first_at_zero=first_at_zero, t_valid=t_valid),
        grid=(b, t // tt),
        in_specs=[pl.BlockSpec((1, tt, c), lambda i, j: (i, j, 0)),
                  pl.BlockSpec((1, tt, c), lambda i, j: (i, j, 2048 // c)),
                  pl.BlockSpec((CONV_W, c), lambda i, j: (0, 0)),
                  vec(),
                  pl.BlockSpec((c, c), lambda i, j: (0, 0)),
                  vec(),
                  pl.BlockSpec((c, c), lambda i, j: (0, 0)),
                  vec(), vec(),
                  pl.BlockSpec((1, SUBLANES, c), lambda i, j: (i, 0, 0)),
                  pl.BlockSpec((1, 1, c), lambda i, j: (i, 0, 0))],
        out_specs=[pl.BlockSpec((1, tt, c), lambda i, j: (i, j, 0)),
                   pl.BlockSpec((1, 1, c), lambda i, j: (i, 0, 0))],
        out_shape=[jax.ShapeDtypeStruct((b, t, c), BF16),
                   jax.ShapeDtypeStruct((b, 1, c), F32)],
        scratch_shapes=[pltpu.VMEM((SUBLANES, c), F32), pltpu.VMEM((1, c), F32)],
        compiler_params=_cparams("parallel", "arbitrary"),
        name="rglru",
    )(p3, p3, wc, bc, wa_bd, ba, wx_bd, bx, lam, conv0, h0)


def _split3(x):
    hi = x.astype(BF16)
    r1 = x - hi.astype(F32)
    mid = r1.astype(BF16)
    lo = (r1 - mid.astype(F32)).astype(BF16)
    return hi, mid, lo


def _l2norm(x):
    return x * lax.rsqrt(jnp.sum(x * x, axis=-1, keepdims=True) + EPS)


def _dn_kernel(uq_ref, uk_ref, uv_ref, z_ref, ba_ref, wc_ref, par_ref, nw_ref, conv0_ref, s0_ref,
               y_ref, st_ref, carry_ref, s_ref, *, tt, t_valid, t_total):
    j = pl.program_id(1)
    nj = pl.num_programs(1)
    hd = DN_HEADS * DN_D

    @pl.when(j == 0)
    def _():
        carry_ref[...] = conv0_ref[0]
        s_ref[...] = s0_ref[0]

    wc = wc_ref[...]
    prev = carry_ref[...]
    parts = []
    for n, ref in enumerate((uq_ref, uk_ref, uv_ref)):
        x = ref[0]
        sl = slice(n * hd, (n + 1) * hd)
        parts.append(_silu(_causal_conv(x, prev[:, sl], wc[:, sl])))
        carry_ref[:, sl] = x[tt - SUBLANES:]
    q_all, k_all, v_all = parts

    ba = ba_ref[0]
    par = par_ref[...]
    beta_all = _sigmoid(ba)
    g_all = -jnp.exp(par[0:1]) * _softplus(ba + par[1:2])
    if t_valid < t_total:
        live = lax.broadcasted_iota(I32, (tt, LANES), 0) + j * tt < t_valid
        beta_all = jnp.where(live, beta_all, 0.0)
        g_all = jnp.where(live, g_all, 0.0)

    ri = lax.broadcasted_iota(I32, (tt, tt), 0)
    ci = lax.broadcasted_iota(I32, (tt, tt), 1)
    lower = ri >= ci
    strict = ri > ci
    tri = jnp.where(lower, 1.0, 0.0).astype(BF16)
    eye = jnp.where(ri == ci, 1.0, 0.0)
    gc_all = sum(_dot(tri, part) for part in _split3(g_all))
    gc_t = jnp.transpose(gc_all)
    gl_row = gc_all[tt - 1:tt]
    eg_all = jnp.exp(gc_all)
    egl_all = jnp.exp(gl_row - gc_all)
    glast_row = jnp.exp(gl_row)
    nw = nw_ref[...]

    heads = range(DN_HEADS)
    hsl = [slice(h * DN_D, (h + 1) * DN_D) for h in heads]
    q_h = [_l2norm(q_all[:, hs]) * (DN_D ** -0.5) for hs in hsl]
    k_h = [_l2norm(k_all[:, hs]) for hs in hsl]
    beta_h = [beta_all[:, h:h + 1] for h in heads]
    k16 = [k.astype(BF16) for k in k_h]
    kk_h = [_dot_nt(k16[h], k16[h]) for h in heads]
    qk_h = [_dot_nt(q_h[h].astype(BF16), k16[h]) for h in heads]
    decay_h = []
    for h in heads:
        gcol = DN_HEADS + h
        dmat = gc_all[:, gcol:gcol + 1] - gc_t[gcol:gcol + 1, :]
        decay_h.append(jnp.exp(jnp.where(lower, dmat, NEG_BIG)))
    x_h = [jnp.where(strict, -(beta_h[h] * kk_h[h] * decay_h[h]), 0.0) for h in heads]
    attn_h = [(qk_h[h] * decay_h[h]).astype(BF16) for h in heads]
    pm_h = [eye + x for x in x_h]
    span = 2
    while span < tt:
        x16 = [x.astype(BF16) for x in x_h]
        x_h = [_dot(x16[h], x16[h]) for h in heads]
        pm_h = [pm_h[h] + _dot(x_h[h].astype(BF16), pm_h[h].astype(BF16)) for h in heads]
        span *= 2
    sol_h = []
    for h in heads:
        eg = eg_all[:, DN_HEADS + h:DN_HEADS + h + 1]
        kbeta = k_h[h] * beta_h[h]
        rhs = jnp.concatenate([v_all[:, hsl[h]] * beta_h[h], kbeta * eg], axis=1).astype(BF16)
        sol_h.append(_dot(pm_h[h].astype(BF16), rhs))
    r_h = []
    for h in heads:
        eg = eg_all[:, DN_HEADS + h:DN_HEADS + h + 1]
        wq = jnp.concatenate([sol_h[h][:, DN_D:], q_h[h] * eg], axis=0).astype(BF16)
        r_h.append(_dot(wq, s_ref[h].astype(BF16)))
    vnew_h = [(sol_h[h][:, :DN_D] - r_h[h][:tt]).astype(BF16) for h in heads]
    o_h = [r_h[h][tt:] + _dot(attn_h[h], vnew_h[h]) for h in heads]
    for h in heads:
        gcol = DN_HEADS + h
        kd = k_h[h] * egl_all[:, gcol:gcol + 1]
        s_ref[h] = s_ref[h] * glast_row[:, gcol:gcol + 1] + _dot(jnp.transpose(kd).astype(BF16), vnew_h[h])
        y = _rms(o_h[h], nw) * _silu(z_ref[0][:, hsl[h]])
        y_ref[0, :, hsl[h]] = y.astype(y_ref.dtype)

    @pl.when(j == nj - 1)
    def _():
        st_ref[0] = s_ref[...]


def gated_delta(p3, wc, par, nw, conv0, s0, *, tt, t_valid):
    b, t, _ = p3.shape
    hd = DN_HEADS * DN_D
    return pl.pallas_call(
        functools.partial(_dn_kernel, tt=tt, t_valid=t_valid, t_total=t),
        grid=(b, t // tt),
        in_specs=[pl.BlockSpec((1, tt, hd), lambda i, j: (i, j, 1)),
                  pl.BlockSpec((1, tt, hd), lambda i, j: (i, j, 2)),
                  pl.BlockSpec((1, tt, hd), lambda i, j: (i, j, 3)),
                  pl.BlockSpec((1, tt, hd), lambda i, j: (i, j, 5)),
                  pl.BlockSpec((1, tt, LANES), lambda i, j: (i, j, 3072 // LANES)),
                  pl.BlockSpec((CONV_W, 3 * hd), lambda i, j: (0, 0)),
                  pl.BlockSpec((SUBLANES, LANES), lambda i, j: (0, 0)),
                  pl.BlockSpec((1, DN_D), lambda i, j: (0, 0)),
                  pl.BlockSpec((1, SUBLANES, 3 * hd), lambda i, j: (i, 0, 0)),
                  pl.BlockSpec((1, DN_HEADS, DN_D, DN_D), lambda i, j: (i, 0, 0, 0))],
        out_specs=[pl.BlockSpec((1, tt, hd), lambda i, j: (i, j, 0)),
                   pl.BlockSpec((1, DN_HEADS, DN_D, DN_D), lambda i, j: (i, 0, 0, 0))],
        out_shape=[jax.ShapeDtypeStruct((b, t, hd), BF16),
                   jax.ShapeDtypeStruct((b, DN_HEADS, DN_D, DN_D), F32)],
        scratch_shapes=[pltpu.VMEM((SUBLANES, 3 * hd), F32),
                        pltpu.VMEM((DN_HEADS, DN_D, DN_D), F32)],
        compiler_params=_cparams("parallel", "arbitrary"),
        name="gated_delta",
    )(p3, p3, p3, p3, p3, wc, par, nw, conv0, s0)


LOW_KEY = INT_MIN + (1 << 23)


def _key_to_float(key):
    bits = key ^ ((key >> 31) & 0x7FFFFFFF)
    return pltpu.bitcast(bits, F32)


def _fold(x, axis, op):
    if axis == 0:
        return op.reduce(x.reshape(x.shape[0] // SUBLANES, SUBLANES, x.shape[1]), axis=0)
    parts = [x[:, s * LANES:(s + 1) * LANES] for s in range(x.shape[1] // LANES)]
    while len(parts) > 1:
        parts = [op(a, b) for a, b in zip(parts[::2], parts[1::2])] + ([parts[-1]] if len(parts) % 2 else [])
    return parts[0]


def _select_topk(sc_ref, thr_ref, n_slots, topk, pos_fn, axis, pos_bits):
    tile = sc_ref.shape[1:]
    acc_shape = (SUBLANES, tile[1]) if axis == 0 else (tile[0], LANES)
    vec_shape = (1, tile[1]) if axis == 0 else (tile[0], 1)

    def reduce_all(fn, op, init, dtype):
        def body(c, acc):
            return op(acc, _fold(fn(sc_ref[c], c), axis, op))

        acc = jnp.full(acc_shape, init, dtype)
        if isinstance(n_slots, int):
            for c in range(n_slots):
                acc = body(c, acc)
        else:
            acc = lax.fori_loop(0, n_slots, body, acc)
        return op.reduce(acc, axis=axis, keepdims=True)

    def count(pred):
        return reduce_all(lambda x, c: jnp.where(pred(x, c), 1, 0), jnp.add, 0, I32)

    def any_row(flag):
        return jnp.max(jnp.where(flag, 1.0, 0.0)) > 0.5

    def bit_body(bi, lo):
        cand = lo + lax.shift_left(jnp.int32(1), 31 - bi)
        cand_f = _key_to_float(cand)
        return jnp.where(count(lambda x, c: x >= cand_f) >= topk, cand, lo)

    lo = lax.fori_loop(0, 32, bit_body, jnp.full(vec_shape, INT_MIN, I32))
    lo = jnp.maximum(lo, LOW_KEY)
    thr0 = _key_to_float(lo)
    thr_ref[...] = thr0
    crowded = (lo > LOW_KEY) & (count(lambda x, c: x >= thr0) > topk)

    @pl.when(any_row(crowded))
    def _():
        def walk_cond(st):
            return st[3]

        def walk(st):
            upper, m_prev, c_prev, _ = st
            active = crowded & (c_prev < topk)
            m = reduce_all(lambda x, c: jnp.where((x >= thr0) & (x < upper), x, -jnp.inf),
                           jnp.maximum, -jnp.inf, F32)
            cnt = count(lambda x, c: x >= m)
            m = jnp.where(active, m, m_prev)
            cnt = jnp.where(active, cnt, c_prev)
            return jnp.where(active, m, upper), m, cnt, any_row(crowded & (cnt < topk))

        upper0 = _key_to_float(lo + 1)
        _, thr, cnt, _ = lax.while_loop(walk_cond, walk, (upper0, thr0, jnp.zeros(vec_shape, I32), True))
        thr = jnp.where(crowded, thr, thr0)
        thr_ref[...] = thr
        tie = crowded & (cnt > topk)

        @pl.when(any_row(tie))
        def _():
            need = topk - count(lambda x, c: x > thr)

            def pos_body(bi, j):
                cand = j + lax.shift_left(jnp.int32(1), pos_bits - 1 - bi)
                below = count(lambda x, c: (x == thr) & (pos_fn(c) < cand))
                return jnp.where(below < need, cand, j)

            jmax = lax.fori_loop(0, pos_bits, pos_body, jnp.zeros(vec_shape, I32))

            def fix(c, carry):
                x = sc_ref[c]
                drop = tie & (x == thr) & (pos_fn(c) > jmax)
                sc_ref[c] = jnp.where(drop, -jnp.inf, x)
                return carry

            if isinstance(n_slots, int):
                for c in range(n_slots):
                    fix(c, 0)
            else:
                lax.fori_loop(0, n_slots, fix, 0)


def _dsa_prompt_kernel(q_ref, qi_ref, wi_ref, k_ref, v_ref, ki_ref, o_ref,
                       sc_ref, thr_ref, vt_ref, qs_ref, qis_ref, si_ref, s_ref, m_ref, l_ref, acc_ref, *, qb, kc, topk):
    i = pl.program_id(1)
    s_len = k_ref.shape[1]
    nc = (i * qb + qb + kc - 1) // kc

    @pl.when(i == 0)
    def _():
        for c in range(s_len // kc):
            vt_ref[c] = jnp.transpose(v_ref[0, c * kc:(c + 1) * kc, :].astype(F32)).astype(BF16)

    q = q_ref[0] * (HEAD_DIM ** -0.5 * LOG2E)
    qi = qi_ref[0]
    for g in range(KV_HEADS):
        for r in range(GROUP):
            hh = g * GROUP + r
            qs_ref[g, r * qb:(r + 1) * qb, :] = q[:, hh * HEAD_DIM:(hh + 1) * HEAD_DIM].astype(BF16)
    for h in range(IDX_HEADS):
        qis_ref[h * qb:(h + 1) * qb, :] = qi[:, h * LANES:h * LANES + IDX_DIM]
    w_t = jnp.transpose(wi_ref[0])
    tpos = i * qb + lax.broadcasted_iota(I32, (kc, qb), 1)
    krow = lax.broadcasted_iota(I32, (kc, qb), 0)

    def index_logits(c):
        start = pl.multiple_of(c * kc, kc)
        keys = ki_ref[0, pl.ds(start, kc), :][:, :IDX_DIM]
        return _dot_nt(keys, qis_ref[...])

    si_ref[...] = index_logits(0)

    def score_body(c, carry):
        s_next = index_logits(jnp.minimum(c + 1, nc - 1))
        s_all = si_ref[...]
        sc = None
        for h in range(IDX_HEADS):
            term = w_t[h:h + 1, :] * jnp.maximum(s_all[:, h * qb:(h + 1) * qb], 0.0)
            sc = term if sc is None else sc + term
        sc_ref[c] = jnp.where(krow + c * kc <= tpos, sc, -jnp.inf)
        si_ref[...] = s_next
        return carry

    lax.fori_loop(0, nc, score_body, 0)

    _select_topk(sc_ref, thr_ref, nc, topk, pos_fn=lambda c: krow + c * kc, axis=0,
                 pos_bits=(s_len - 1).bit_length())
    thr = thr_ref[...]

    m_ref[...] = jnp.full(m_ref.shape, NEG_BIG, F32)
    l_ref[...] = jnp.zeros(l_ref.shape, F32)
    acc_ref[...] = jnp.zeros(acc_ref.shape, F32)

    def logits(c, g):
        start = pl.multiple_of(c * kc, kc)
        return _dot_nt(k_ref[0, pl.ds(start, kc), g * HEAD_DIM:(g + 1) * HEAD_DIM], qs_ref[g])

    def probs(g, s, bias_g):
        s = s + bias_g
        m_old = m_ref[g]
        m_new = jnp.maximum(m_old, jnp.max(s, axis=0, keepdims=True))
        p = jnp.exp2(s - m_new)
        alpha = jnp.exp2(m_old - m_new)
        l_ref[g] = alpha * l_ref[g] + jnp.sum(p, axis=0, keepdims=True)
        m_ref[g] = m_new
        return p.astype(BF16), alpha

    s_ref[...] = logits(0, 0)

    def attn_body(c, carry):
        s1 = logits(c, 1)
        s_next = logits(jnp.minimum(c + 1, nc - 1), 0)
        bias = jnp.where(sc_ref[c] >= thr, 0.0, NEG_BIG)
        bias_g = jnp.concatenate([bias] * GROUP, axis=1)
        vtc = vt_ref[c]
        p0, alpha0 = probs(0, s_ref[...], bias_g)
        p1, alpha1 = probs(1, s1, bias_g)
        acc_ref[0] = alpha0 * acc_ref[0] + _dot(vtc[:HEAD_DIM], p0)
        acc_ref[1] = alpha1 * acc_ref[1] + _dot(vtc[HEAD_DIM:], p1)
        s_ref[...] = s_next
        return carry

    lax.fori_loop(0, nc, attn_body, 0)
    for g in range(KV_HEADS):
        out_t = acc_ref[g] / l_ref[g]
        for r in range(GROUP):
            hh = g * GROUP + r
            o_ref[0, :, hh * HEAD_DIM:(hh + 1) * HEAD_DIM] = jnp.transpose(
                out_t[:, r * qb:(r + 1) * qb]).astype(o_ref.dtype)


ATT_Q0 = 0
ATT_QI0 = 1024
ATT_K0 = 2048
ATT_V0 = 2304
ATT_KI0 = 2560
ATT_WI0 = 2688
ATT_NP = 2816


def dsa_prompt(p3, pb3, *, qb, kc):
    b, s, _ = p3.shape
    topk = min(TOPK_MAX, s // 4)
    hw = ATT_HEADS * HEAD_DIM
    kvw = KV_HEADS * HEAD_DIM
    return pl.pallas_call(
        functools.partial(_dsa_prompt_kernel, qb=qb, kc=kc, topk=topk),
        grid=(b, s // qb),
        in_specs=[pl.BlockSpec((1, qb, hw), lambda bi, i: (bi, i, ATT_Q0 // hw)),
                  pl.BlockSpec((1, qb, IDX_HEADS * LANES), lambda bi, i: (bi, i, ATT_QI0 // (IDX_HEADS * LANES))),
                  pl.BlockSpec((1, qb, LANES), lambda bi, i: (bi, i, ATT_WI0 // LANES)),
                  pl.BlockSpec((1, s, kvw), lambda bi, i: (bi, 0, ATT_K0 // kvw)),
                  pl.BlockSpec((1, s, kvw), lambda bi, i: (bi, 0, ATT_V0 // kvw)),
                  pl.BlockSpec((1, s, LANES), lambda bi, i: (bi, 0, ATT_KI0 // LANES))],
        out_specs=pl.BlockSpec((1, qb, hw), lambda bi, i: (bi, i, 0)),
        out_shape=jax.ShapeDtypeStruct((b, s, hw), BF16),
        scratch_shapes=[pltpu.VMEM((s // kc, kc, qb), F32),
                        pltpu.VMEM((1, qb), F32),
                        pltpu.VMEM((s // kc, kvw, kc), BF16),
                        pltpu.VMEM((KV_HEADS, GROUP * qb, HEAD_DIM), BF16),
                        pltpu.VMEM((IDX_HEADS * qb, IDX_DIM), BF16),
                        pltpu.VMEM((kc, IDX_HEADS * qb), F32),
                        pltpu.VMEM((kc, GROUP * qb), F32),
                        pltpu.VMEM((KV_HEADS, 1, GROUP * qb), F32),
                        pltpu.VMEM((KV_HEADS, 1, GROUP * qb), F32),
                        pltpu.VMEM((KV_HEADS, HEAD_DIM, GROUP * qb), F32)],
        compiler_params=_cparams("parallel", "arbitrary"),
        name="dsa_prompt",
    )(p3, pb3, p3, pb3, pb3, pb3)


def _index_scores(qi, wi, keys, keys_transposed):
    sc = None
    for h in range(IDX_HEADS):
        qh = qi[:, h * LANES:h * LANES + IDX_DIM]
        s = _dot(qh, keys) if keys_transposed else _dot_nt(qh, keys)
        term = wi[:, h:h + 1] * jnp.maximum(s, 0.0)
        sc = term if sc is None else sc + term
    return sc


def _dsa_sample_select_kernel(pt_ref, qi_ref, wi_ref, kin_ref, *rest, pages, nj, t_new, topk):
    page_refs = rest[:pages]
    bias_ref = rest[pages]
    sc_ref, thr_ref = rest[pages + 1:]
    j = pl.program_id(1)
    rows = qi_ref.shape[1]
    width = pages * LANES
    qi = qi_ref[0].astype(BF16)
    wi = wi_ref[0]
    keys_t = jnp.concatenate([r[...] for r in page_refs], axis=1).astype(BF16)
    sc_ref[j] = _index_scores(qi, wi, keys_t, True)

    @pl.when(j == nj - 1)
    def _():
        kin = kin_ref[0][:, :IDX_DIM].astype(BF16)
        kin = jnp.concatenate([kin, jnp.zeros((LANES - rows, IDX_DIM), BF16)], axis=0)
        snew = _index_scores(qi, wi, kin, False)
        ri = lax.broadcasted_iota(I32, (rows, LANES), 0)
        ci = lax.broadcasted_iota(I32, (rows, LANES), 1)
        snew = jnp.where((ci <= ri) & (ci < t_new), snew, -jnp.inf)
        sc_ref[nj] = jnp.concatenate([snew, jnp.full((rows, width - LANES), -jnp.inf, F32)], axis=1)
        lane = lax.broadcasted_iota(I32, (rows, width), 1)
        _select_topk(sc_ref, thr_ref, nj + 1, topk, pos_fn=lambda c: lane + c * width, axis=1,
                     pos_bits=((nj + 1) * width - 1).bit_length())
        thr = thr_ref[...]
        for c in range(nj + 1):
            bias_ref[0, c] = jnp.where(sc_ref[c] >= thr, 0.0, NEG_BIG)


def dsa_sample_select(ps8, cache_kidx_t, layer, page_table, *, pages, t_new):
    db, rows, _ = ps8.shape
    n_pages = page_table.shape[1]
    page = cache_kidx_t.shape[3]
    nj = n_pages // pages
    width = pages * LANES
    topk = min(TOPK_MAX, (n_pages * page + t_new) // 4)

    def page_spec(r):
        return pl.BlockSpec((None, None, IDX_DIM, page),
                            lambda bi, j, pt: (layer, pt[bi * n_pages + j * pages + r], 0, 0))

    grid_spec = pltpu.PrefetchScalarGridSpec(
        num_scalar_prefetch=1,
        grid=(db, nj),
        in_specs=[pl.BlockSpec((1, rows, IDX_HEADS * LANES), lambda bi, j, pt: (bi, 0, ATT_QI0 // (IDX_HEADS * LANES))),
                  pl.BlockSpec((1, rows, LANES), lambda bi, j, pt: (bi, 0, ATT_WI0 // LANES)),
                  pl.BlockSpec((1, rows, LANES), lambda bi, j, pt: (bi, 0, ATT_KI0 // LANES))]
        + [page_spec(r) for r in range(pages)],
        out_specs=pl.BlockSpec((1, nj + 1, rows, width), lambda bi, j, pt: (bi, 0, 0, 0)),
        scratch_shapes=[pltpu.VMEM((nj + 1, rows, width), F32), pltpu.VMEM((rows, 1), F32)],
    )
    return pl.pallas_call(
        functools.partial(_dsa_sample_select_kernel, pages=pages, nj=nj, t_new=t_new, topk=topk),
        grid_spec=grid_spec,
        out_shape=jax.ShapeDtypeStruct((db, nj + 1, rows, width), F32),
        compiler_params=_cparams("parallel", "arbitrary"),
        name="dsa_sample_select",
    )(page_table.reshape(-1), ps8, ps8, ps8, *([cache_kidx_t] * pages))


def _softmax_update(s, m_old, l_old, acc_old, vals):
    m_new = jnp.maximum(m_old, jnp.max(s, axis=1, keepdims=True))
    p = jnp.exp(s - m_new)
    alpha = jnp.exp(m_old - m_new)
    l_new = alpha * l_old + jnp.sum(p, axis=1, keepdims=True)
    acc_new = alpha * acc_old + _dot(p.astype(BF16), vals)
    return m_new, l_new, acc_new


def _dsa_sample_attend_kernel(pt_ref, q_ref, kn_ref, vn_ref, bias_ref, *rest, pages):
    k_refs = rest[:pages]
    v_refs = rest[pages:2 * pages]
    o_ref, qs_ref, m_ref, l_ref, acc_ref = rest[2 * pages:]
    j = pl.program_id(1)
    nj = pl.num_programs(1)
    rows = q_ref.shape[1]
    page = k_refs[0].shape[0] // KV_HEADS

    @pl.when(j == 0)
    def _():
        q = q_ref[0] * (HEAD_DIM ** -0.5)
        for g in range(KV_HEADS):
            for r in range(GROUP):
                hh = g * GROUP + r
                qs_ref[g, r * rows:(r + 1) * rows, :] = q[:, hh * HEAD_DIM:(hh + 1) * HEAD_DIM].astype(BF16)
        m_ref[...] = jnp.full(m_ref.shape, NEG_BIG, F32)
        l_ref[...] = jnp.zeros(l_ref.shape, F32)
        acc_ref[...] = jnp.zeros(acc_ref.shape, F32)

    def update(g, kch, vch, bias4):
        s = _dot_nt(qs_ref[g], kch) + bias4
        m_new, l_new, acc_new = _softmax_update(s, m_ref[g], l_ref[g], acc_ref[g], vch)
        m_ref[g] = m_new
        l_ref[g] = l_new
        acc_ref[g] = acc_new

    bias4 = jnp.concatenate([bias_ref[0, j]] * GROUP, axis=0)
    for g in range(KV_HEADS):
        head_rows = pl.ds(g, page, stride=KV_HEADS)
        kch = jnp.concatenate([r[head_rows, :] for r in k_refs], axis=0).astype(BF16)
        vch = jnp.concatenate([r[head_rows, :] for r in v_refs], axis=0).astype(BF16)
        update(g, kch, vch, bias4)

    @pl.when(j == nj - 1)
    def _():
        pad = jnp.zeros((LANES - rows, HEAD_DIM), BF16)
        bias_new = jnp.concatenate([bias_ref[0, nj][:, :LANES]] * GROUP, axis=0)
        for g in range(KV_HEADS):
            gs = slice(g * HEAD_DIM, (g + 1) * HEAD_DIM)
            kn = jnp.concatenate([kn_ref[0][:, gs].astype(BF16), pad], axis=0)
            vn = jnp.concatenate([vn_ref[0][:, gs].astype(BF16), pad], axis=0)
            update(g, kn, vn, bias_new)
            out = acc_ref[g] / l_ref[g]
            for r in range(GROUP):
                hh = g * GROUP + r
                o_ref[0, :, hh * HEAD_DIM:(hh + 1) * HEAD_DIM] = out[r * rows:(r + 1) * rows].astype(o_ref.dtype)


def dsa_sample_attend(ps8, bias, cache_k, cache_v, layer, page_table, *, pages):
    db, rows, _ = ps8.shape
    n_pages = page_table.shape[1]
    page = cache_k.shape[2] // KV_HEADS
    nj = n_pages // pages
    width = pages * LANES
    hw = ATT_HEADS * HEAD_DIM
    kvw = KV_HEADS * HEAD_DIM

    def page_spec(r):
        return pl.BlockSpec((None, None, page * KV_HEADS, HEAD_DIM),
                            lambda bi, j, pt: (layer, pt[bi * n_pages + j * pages + r], 0, 0))

    grid_spec = pltpu.PrefetchScalarGridSpec(
        num_scalar_prefetch=1,
        grid=(db, nj),
        in_specs=[pl.BlockSpec((1, rows, hw), lambda bi, j, pt: (bi, 0, ATT_Q0 // hw)),
                  pl.BlockSpec((1, rows, kvw), lambda bi, j, pt: (bi, 0, ATT_K0 // kvw)),
                  pl.BlockSpec((1, rows, kvw), lambda bi, j, pt: (bi, 0, ATT_V0 // kvw)),
                  pl.BlockSpec((1, nj + 1, rows, width), lambda bi, j, pt: (bi, 0, 0, 0))]
        + [page_spec(r) for r in range(pages)] * 2,
        out_specs=pl.BlockSpec((1, rows, hw), lambda bi, j, pt: (bi, 0, 0)),
        scratch_shapes=[pltpu.VMEM((KV_HEADS, GROUP * rows, HEAD_DIM), BF16),
                        pltpu.VMEM((KV_HEADS, GROUP * rows, 1), F32),
                        pltpu.VMEM((KV_HEADS, GROUP * rows, 1), F32),
                        pltpu.VMEM((KV_HEADS, GROUP * rows, HEAD_DIM), F32)],
    )
    return pl.pallas_call(
        functools.partial(_dsa_sample_attend_kernel, pages=pages),
        grid_spec=grid_spec,
        out_shape=jax.ShapeDtypeStruct((db, rows, hw), BF16),
        compiler_params=_cparams("parallel", "arbitrary"),
        name="dsa_sample_attend",
    )(page_table.reshape(-1), ps8, ps8, ps8, bias, *([cache_k] * pages), *([cache_v] * pages))


REC_NP = 3200
SAMPLE_PAD_T = 128
SAMPLE_ROWS = 8
PAGES_PER_STEP = 16


def _block_diag(w):
    h, a, b = w.shape
    return jnp.einsum("hab,hg->hagb", w, jnp.eye(h, dtype=w.dtype)).reshape(h * a, h * b)


def _rec_weights(w_in16, w_conv, b_conv, wa16, ba, wx16, bx, lam, a_log, dt_bias, dn_w, w_out16):
    c = ba.shape[0]
    par = jnp.zeros((SUBLANES, LANES), F32)
    par = par.at[0, DN_HEADS:2 * DN_HEADS].set(a_log).at[1, DN_HEADS:2 * DN_HEADS].set(dt_bias)
    return dict(
        w_in=jnp.pad(w_in16, ((0, 0), (0, REC_NP - w_in16.shape[1]))),
        wc_rg=w_conv[:, :c], wc_dn=w_conv[:, c:], bc=b_conv.reshape(1, c),
        wa=_block_diag(wa16), ba=ba.reshape(1, c), wx=_block_diag(wx16),
        bx=bx.reshape(1, c), lam=lam.reshape(1, c), par=par, nw=dn_w.reshape(1, -1),
        w_out_rg=w_out16[:c], w_out_dn=w_out16[c:])


def _att_weights(w_in16, w_out16):
    d = w_in16.shape[0]
    hw = ATT_HEADS * HEAD_DIM
    kvw = KV_HEADS * HEAD_DIM
    o1, o3 = hw, hw + 2 * kvw
    o4 = o3 + IDX_HEADS * IDX_DIM
    o5 = o4 + IDX_DIM
    qi = w_in16[:, o3:o4].reshape(d, IDX_HEADS, IDX_DIM)
    qi = jnp.pad(qi, ((0, 0), (0, 0), (0, LANES - IDX_DIM))).reshape(d, IDX_HEADS * LANES)
    ki = jnp.pad(w_in16[:, o4:o5], ((0, 0), (0, LANES - IDX_DIM)))
    wi = jnp.pad(w_in16[:, o5:], ((0, 0), (0, LANES - IDX_HEADS)))
    return dict(w_in=jnp.concatenate([w_in16[:, :o1], qi, w_in16[:, o1:o3], ki, wi], axis=1), w_out=w_out16)


def _rec_layer(x2d, b, t, wts, g, conv_state, h_state, s_state, *, tm, tt, first_at_zero, t_valid):
    c = wts["bc"].shape[1]
    p = norm_matmul(x2d, g, wts["w_in"], tm=tm)
    p3 = p.reshape(b, t_valid, REC_NP)
    conv_new = p3[:, t_valid - (CONV_W - 1):, :conv_state.shape[-1]]
    if t != t_valid:
        p3 = jnp.pad(p3, ((0, 0), (0, t - t_valid), (0, 0)))
    conv0 = jnp.pad(conv_state, ((0, 0), (SUBLANES - (CONV_W - 1), 0), (0, 0)))
    rg_y, h_t = rglru(p3, wts["wc_rg"], wts["bc"], wts["wa"], wts["ba"], wts["wx"], wts["bx"], wts["lam"],
                      conv0[:, :, :c], h_state.reshape(b, 1, c), tt=tt, first_at_zero=first_at_zero,
                      t_valid=t_valid)
    dn_y, s_t = gated_delta(p3, wts["wc_dn"], wts["par"], wts["nw"], conv0[:, :, c:], s_state,
                            tt=tt, t_valid=t_valid)
    if t != t_valid:
        rg_y = rg_y[:, :t_valid]
        dn_y = dn_y[:, :t_valid]
    acts = [rg_y.reshape(b * t_valid, -1), dn_y.reshape(b * t_valid, -1)]
    return acts, [wts["w_out_rg"], wts["w_out_dn"]], conv_new, h_t.reshape(b, c), s_t


def _split_kv(p3):
    b, t, _ = p3.shape
    kvw = KV_HEADS * HEAD_DIM
    k = p3[:, :, ATT_K0:ATT_K0 + kvw].reshape(b, t, KV_HEADS, HEAD_DIM)
    v = p3[:, :, ATT_V0:ATT_V0 + kvw].reshape(b, t, KV_HEADS, HEAD_DIM)
    ki = p3[:, :, ATT_KI0:ATT_KI0 + IDX_DIM]
    return k, v, ki


def kernel(x_prompt, x_sample, state_conv, state_h, state_S, cache_k, cache_v, cache_kidx, page_table,
           norm_mix, norm_mlp, norm_final, w_in_rec, w_conv, b_conv, rg_wa, rg_ba, rg_wx, rg_bx,
           rg_lambda, dn_a_log, dn_dt_bias, dn_norm, w_out_rec, w_in_att, w_out_att, w_up, w_down):
    bsz, seq, d = x_prompt.shape
    db, t_new, _ = x_sample.shape
    depth = norm_mix.shape[0]
    past = page_table.shape[1] * cache_k.shape[2]
    xp = x_prompt.reshape(bsz * seq, d)
    xs = x_sample.reshape(db * t_new, d)
    tm_p = 256
    tm_s = db * t_new
    cache_kidx_t = jnp.swapaxes(cache_kidx, 2, 3)
    cache_k4 = cache_k.reshape(cache_k.shape[:2] + (-1, HEAD_DIM))
    cache_v4 = cache_v.reshape(cache_v.shape[:2] + (-1, HEAD_DIM))
    w_up16 = cast_bf16(w_up)
    w_down16 = cast_bf16(w_down)
    w_in_rec16 = cast_bf16(w_in_rec)
    w_out_rec16 = cast_bf16(w_out_rec)
    rg_wa16 = cast_bf16(rg_wa)
    rg_wx16 = cast_bf16(rg_wx)
    w_in_att16 = cast_bf16(w_in_att)
    w_out_att16 = cast_bf16(w_out_att)

    conv_p, h_p, s_p, conv_s, h_s, s_s = [], [], [], [], [], []
    k_p, v_p, ki_p, k_s, v_s, ki_s = [], [], [], [], [], []
    for layer in range(depth):
        if layer % 2 == 0:
            r = layer // 2
            wts = _rec_weights(w_in_rec16[r], w_conv[r], b_conv[r], rg_wa16[r], rg_ba[r], rg_wx16[r], rg_bx[r],
                               rg_lambda[r], dn_a_log[r], dn_dt_bias[r], dn_norm[r], w_out_rec16[r])
            conv_ch = state_conv.shape[-1]
            acts_p, ws_p, cb, ht, st = _rec_layer(
                xp, bsz, seq, wts, norm_mix[layer], jnp.zeros((bsz, CONV_W - 1, conv_ch), F32),
                jnp.zeros((bsz, rg_ba.shape[1]), F32), jnp.zeros((bsz,) + state_S.shape[2:], F32),
                tm=tm_p, tt=256, first_at_zero=True, t_valid=seq)
            conv_p.append(cb); h_p.append(ht); s_p.append(st)
            acts_s, ws_s, cb, ht, st = _rec_layer(
                xs, db, SAMPLE_PAD_T, wts, norm_mix[layer], state_conv[r], state_h[r], state_S[r],
                tm=tm_s, tt=SAMPLE_PAD_T, first_at_zero=(past == 0), t_valid=t_new)
            conv_s.append(cb); h_s.append(ht); s_s.append(st)
        else:
            a = layer // 2
            wts = _att_weights(w_in_att16[a], w_out_att16[a])
            p, pb = norm_matmul(xp, norm_mix[layer], wts["w_in"], tm=tm_p, with_bf16=True)
            p3 = p.reshape(bsz, seq, ATT_NP)
            o = dsa_prompt(p3, pb.reshape(bsz, seq, ATT_NP), qb=128, kc=512)
            acts_p, ws_p = [o.reshape(bsz * seq, -1)], [wts["w_out"]]
            kk, vv, kii = _split_kv(p3)
            k_p.append(kk); v_p.append(vv); ki_p.append(kii)

            ps = norm_matmul(xs, norm_mix[layer], wts["w_in"], tm=tm_s)
            ps3 = ps.reshape(db, t_new, ATT_NP)
            ps8 = jnp.pad(ps3, ((0, 0), (0, SAMPLE_ROWS - t_new), (0, 0)))
            bias = dsa_sample_select(ps8, cache_kidx_t, a, page_table, pages=PAGES_PER_STEP, t_new=t_new)
            o = dsa_sample_attend(ps8, bias, cache_k4, cache_v4, a, page_table, pages=PAGES_PER_STEP)
            acts_s, ws_s = [o[:, :t_new].reshape(db * t_new, -1)], [wts["w_out"]]
            kk, vv, kii = _split_kv(ps3)
            k_s.append(kk); v_s.append(vv); ki_s.append(kii)
        xp = proj_mlp(acts_p, ws_p, xp, norm_mlp[layer], w_up16[layer], w_down16[layer], tm=512, tf=1024)
        xs = proj_mlp(acts_s, ws_s, xs, norm_mlp[layer], w_up16[layer], w_down16[layer], tm=tm_s, tf=1024)
    y_prompt = final_norm(xp, norm_final, tm=512).reshape(bsz, seq, d)
    y_sample = final_norm(xs, norm_final, tm=tm_s).reshape(db, t_new, d)
    return (y_prompt, y_sample,
            jnp.stack(conv_p), jnp.stack(h_p), jnp.stack(s_p),
            jnp.stack(k_p), jnp.stack(v_p), jnp.stack(ki_p),
            jnp.stack(conv_s), jnp.stack(h_s), jnp.stack(s_s),
            jnp.stack(k_s), jnp.stack(v_s), jnp.stack(ki_s))
```

```python
import functools

import jax
import jax.numpy as jnp
from jax import lax
from jax.experimental import pallas as pl
from jax.experimental.pallas import tpu as pltpu

F32 = jnp.float32
BF16 = jnp.bfloat16
I32 = jnp.int32

EPS = 1e-6
CONV_W = 4
D_RNN_HEADS = 8
RG_C = 8.0
DN_HEADS = 4
DN_D = 128
ATT_HEADS = 8
KV_HEADS = 2
GROUP = ATT_HEADS // KV_HEADS
HEAD_DIM = 128
IDX_HEADS = 8
IDX_DIM = 64
TOPK_MAX = 256
LANES = 128
SUBLANES = 8
VMEM_LIMIT = 48 * 1024 * 1024
INT_MIN = -2 ** 31
NEG_BIG = -1e30
LOG2E = 1.4426950408889634
CAST_TILE_ELEMS = 1 << 21


def _cparams(*sem):
    return pltpu.CompilerParams(dimension_semantics=sem, vmem_limit_bytes=VMEM_LIMIT)


def _sigmoid(x):
    return 1.0 / (1.0 + jnp.exp(-x))


def _softplus(x):
    return jnp.maximum(x, 0.0) + jnp.log(1.0 + jnp.exp(-jnp.abs(x)))


def _silu(x):
    return x * _sigmoid(x)


def _gelu_tanh(x):
    return 0.5 * x * (1.0 + jnp.tanh(0.7978845608028654 * (x + 0.044715 * x * x * x)))


def _rms(x, g):
    return x * lax.rsqrt(jnp.mean(x * x, axis=-1, keepdims=True) + EPS) * g


def _dot(a, b):
    return jnp.dot(a, b, preferred_element_type=F32)


def _dot_nt(a, b):
    return lax.dot_general(a, b, (((1,), (1,)), ((), ())), preferred_element_type=F32)


def _norm_matmul_kernel(x_ref, g_ref, w_ref, o_ref, *rest):
    xn = _rms(x_ref[...], g_ref[...]).astype(BF16)
    y = _dot(xn, w_ref[...])
    o_ref[...] = y
    if rest:
        rest[0][...] = y.astype(BF16)


def norm_matmul(x, g, w, *, tm, with_bf16=False):
    m, d = x.shape
    n = w.shape[1]
    out_shape = [jax.ShapeDtypeStruct((m, n), F32)]
    out_specs = [pl.BlockSpec((tm, n), lambda i: (i, 0))]
    if with_bf16:
        out_shape.append(jax.ShapeDtypeStruct((m, n), BF16))
        out_specs.append(pl.BlockSpec((tm, n), lambda i: (i, 0)))
    res = pl.pallas_call(
        _norm_matmul_kernel,
        grid=(m // tm,),
        in_specs=[pl.BlockSpec((tm, d), lambda i: (i, 0)),
                  pl.BlockSpec((1, d), lambda i: (0, 0)),
                  pl.BlockSpec((d, n), lambda i: (0, 0))],
        out_specs=out_specs,
        out_shape=out_shape,
        compiler_params=_cparams("parallel"),
        name="norm_matmul",
    )(x, g.reshape(1, d), w)
    return res if with_bf16 else res[0]


def _proj_mlp_kernel(*refs, n_in):
    a_refs = refs[:n_in]
    w_refs = refs[n_in:2 * n_in]
    x_ref, g_ref, wu_ref, wd_ref, o_ref, xn_ref = refs[2 * n_in:]
    f = pl.program_id(1)

    @pl.when(f == 0)
    def _():
        x = x_ref[...]
        for a_ref, w_ref in zip(a_refs, w_refs):
            x = x + _dot(a_ref[...].astype(BF16), w_ref[...])
        xn_ref[...] = _rms(x, g_ref[...]).astype(BF16)
        o_ref[...] = x

    h = jnp.maximum(_dot(xn_ref[...], wu_ref[...]), 0.0)
    o_ref[...] += _dot((h * h).astype(BF16), wd_ref[...])


def proj_mlp(acts, ws, x, g, w_up, w_down, *, tm, tf):
    m, d = x.shape
    dff = w_up.shape[1]
    n_in = len(acts)
    in_specs = [pl.BlockSpec((tm, a.shape[1]), lambda i, f: (i, 0)) for a in acts]
    in_specs += [pl.BlockSpec(w.shape, lambda i, f: (0, 0)) for w in ws]
    in_specs += [pl.BlockSpec((tm, d), lambda i, f: (i, 0)),
                 pl.BlockSpec((1, d), lambda i, f: (0, 0)),
                 pl.BlockSpec((d, tf), lambda i, f: (0, f)),
                 pl.BlockSpec((tf, d), lambda i, f: (f, 0))]
    return pl.pallas_call(
        functools.partial(_proj_mlp_kernel, n_in=n_in),
        grid=(m // tm, dff // tf),
        in_specs=in_specs,
        out_specs=pl.BlockSpec((tm, d), lambda i, f: (i, 0)),
        out_shape=jax.ShapeDtypeStruct((m, d), F32),
        scratch_shapes=[pltpu.VMEM((tm, d), BF16)],
        compiler_params=_cparams("parallel", "arbitrary"),
        name="proj_mlp",
    )(*acts, *ws, x, g.reshape(1, d), w_up, w_down)


def _final_norm_kernel(x_ref, g_ref, o_ref):
    o_ref[...] = _rms(x_ref[...], g_ref[...])


def final_norm(x, g, *, tm):
    m, d = x.shape
    return pl.pallas_call(
        _final_norm_kernel,
        grid=(m // tm,),
        in_specs=[pl.BlockSpec((tm, d), lambda i: (i, 0)),
                  pl.BlockSpec((1, d), lambda i: (0, 0))],
        out_specs=pl.BlockSpec((tm, d), lambda i: (i, 0)),
        out_shape=jax.ShapeDtypeStruct((m, d), F32),
        compiler_params=_cparams("parallel"),
        name="final_norm",
    )(x, g.reshape(1, d))


def _cast_kernel(x_ref, o_ref):
    o_ref[...] = x_ref[...].astype(o_ref.dtype)


def cast_bf16(w):
    n = w.shape[-1]
    w2 = w.reshape(-1, n)
    m = w2.shape[0]
    tm = m
    while tm * n > CAST_TILE_ELEMS and tm % (4 * SUBLANES) == 0:
        tm //= 2
    out = pl.pallas_call(
        _cast_kernel,
        grid=(m // tm,),
        in_specs=[pl.BlockSpec((tm, n), lambda i: (i, 0))],
        out_specs=pl.BlockSpec((tm, n), lambda i: (i, 0)),
        out_shape=jax.ShapeDtypeStruct((m, n), BF16),
        compiler_params=_cparams("parallel"),
        name="cast_bf16",
    )(w2)
    return out.reshape(w.shape)


def _causal_conv(x, prev, wc):
    tt, c = x.shape
    row = lax.broadcasted_iota(I32, (SUBLANES, c), 0)
    acc = x * wc[CONV_W - 1:CONV_W]
    for k in range(1, CONV_W):
        r = pltpu.roll(x, k, 0)
        top = jnp.where(row < k, pltpu.roll(prev, k, 0), r[:SUBLANES])
        r = jnp.concatenate([top, r[SUBLANES:]], axis=0)
        acc = acc + r * wc[CONV_W - 1 - k:CONV_W - k]
    return acc


def _rglru_kernel(u_ref, gate_ref, wc_ref, bc_ref, wa_ref, ba_ref, wx_ref, bx_ref, lam_ref,
                  conv0_ref, h0_ref, y_ref, ht_ref, carry_ref, h_ref, *, tt, first_at_zero, t_valid):
    j = pl.program_id(1)

    @pl.when(j == 0)
    def _():
        carry_ref[...] = conv0_ref[0]
        h_ref[...] = h0_ref[0]

    x = u_ref[0]
    c = x.shape[1]
    xr = _causal_conv(x, carry_ref[...], wc_ref[...]) + bc_ref[...]
    carry_ref[...] = x[tt - SUBLANES:]
    xb = xr.astype(BF16)
    r_g = _sigmoid(_dot(xb, wa_ref[...]) + ba_ref[...])
    i_g = _sigmoid(_dot(xb, wx_ref[...]) + bx_ref[...])
    log_a = -RG_C * r_g * _softplus(-lam_ref[...])
    a = jnp.exp(log_a)
    mult = jnp.sqrt(jnp.maximum(1.0 - jnp.exp(2.0 * log_a), 0.0))
    row = lax.broadcasted_iota(I32, (tt, c), 0)
    if first_at_zero:
        mult = jnp.where(row + j * tt == 0, 1.0, mult)
    b = xr * i_g * mult
    k = 1
    while k < tt:
        keep = row >= k
        a_s = jnp.where(keep, pltpu.roll(a, k, 0), 1.0)
        b_s = jnp.where(keep, pltpu.roll(b, k, 0), 0.0)
        b = a * b_s + b
        a = a * a_s
        k *= 2
    h = a * h_ref[...] + b
    h_ref[...] = h[tt - 1:tt]
    y_ref[0] = (h * _gelu_tanh(gate_ref[0])).astype(y_ref.dtype)
    j_last, r_last = divmod(t_valid - 1, tt)

    @pl.when(j == j_last)
    def _():
        ht_ref[0] = h[r_last:r_last + 1]


def rglru(p3, wc, bc, wa_bd, ba, wx_bd, bx, lam, conv0, h0, *, tt, first_at_zero, t_valid):
    b, t, _ = p3.shape
    c = wc.shape[1]
    vec = lambda: pl.BlockSpec((1, c), lambda i, j: (0, 0))
    return pl.pallas_call(
        functools.partial(_rglru_kernel, tt=tt, first_at_zero=first_at_zero, t_valid=t_valid),
        grid=(b, t // tt),
        in_specs=[pl.BlockSpec((1, tt, c), lambda i, j: (i, j, 0)),
                  pl.BlockSpec((1, tt, c), lambda i, j: (i, j, 2048 // c)),
                  pl.BlockSpec((CONV_W, c), lambda i, j: (0, 0)),
                  vec(),
                  pl.BlockSpec((c, c), lambda i, j: (0, 0)),
                  vec(),
                  pl.BlockSpec((c, c), lambda i, j: (0, 0)),
                  vec(), vec(),
                  pl.BlockSpec((1, SUBLANES, c), lambda i, j: (i, 0, 0)),
                  pl.BlockSpec((1, 1, c), lambda i, j: (i, 0, 0))],
        out_specs=[pl.BlockSpec((1, tt, c), lambda i, j: (i, j, 0)),
                   pl.BlockSpec((1, 1, c), lambda i, j: (i, 0, 0))],
        out_shape=[jax.ShapeDtypeStruct((b, t, c), BF16),
                   jax.ShapeDtypeStruct((b, 1, c), F32)],
        scratch_shapes=[pltpu.VMEM((SUBLANES, c), F32), pltpu.VMEM((1, c), F32)],
        compiler_params=_cparams("parallel", "arbitrary"),
        name="rglru",
    )(p3, p3, wc, bc, wa_bd, ba, wx_bd, bx, lam, conv0, h0)


def _split3(x):
    hi = x.astype(BF16)
    r1 = x - hi.astype(F32)
    mid = r1.astype(BF16)
    lo = (r1 - mid.astype(F32)).astype(BF16)
    return hi, mid, lo


def _l2norm(x):
    return x * lax.rsqrt(jnp.sum(x * x, axis=-1, keepdims=True) + EPS)


def _dn_kernel(uq_ref, uk_ref, uv_ref, z_ref, ba_ref, wc_ref, par_ref, nw_ref, conv0_ref, s0_ref,
               y_ref, st_ref, carry_ref, s_ref, *, tt, t_valid, t_total):
    j = pl.program_id(1)
    nj = pl.num_programs(1)
    hd = DN_HEADS * DN_D

    @pl.when(j == 0)
    def _():
        carry_ref[...] = conv0_ref[0]
        s_ref[...] = s0_ref[0]

    wc = wc_ref[...]
    prev = carry_ref[...]
    parts = []
    for n, ref in enumerate((uq_ref, uk_ref, uv_ref)):
        x = ref[0]
        sl = slice(n * hd, (n + 1) * hd)
        parts.append(_silu(_causal_conv(x, prev[:, sl], wc[:, sl])))
        carry_ref[:, sl] = x[tt - SUBLANES:]
    q_all, k_all, v_all = parts

    ba = ba_ref[0]
    par = par_ref[...]
    beta_all = _sigmoid(ba)
    g_all = -jnp.exp(par[0:1]) * _softplus(ba + par[1:2])
    if t_valid < t_total:
        live = lax.broadcasted_iota(I32, (tt, LANES), 0) + j * tt < t_valid
        beta_all = jnp.where(live, beta_all, 0.0)
        g_all = jnp.where(live, g_all, 0.0)

    ri = lax.broadcasted_iota(I32, (tt, tt), 0)
    ci = lax.broadcasted_iota(I32, (tt, tt), 1)
    lower = ri >= ci
    strict = ri > ci
    tri = jnp.where(lower, 1.0, 0.0).astype(BF16)
    eye = jnp.where(ri == ci, 1.0, 0.0)
    gc_all = sum(_dot(tri, part) for part in _split3(g_all))
    gc_t = jnp.transpose(gc_all)
    gl_row = gc_all[tt - 1:tt]
    eg_all = jnp.exp(gc_all)
    egl_all = jnp.exp(gl_row - gc_all)
    glast_row = jnp.exp(gl_row)
    nw = nw_ref[...]

    heads = range(DN_HEADS)
    hsl = [slice(h * DN_D, (h + 1) * DN_D) for h in heads]
    q_h = [_l2norm(q_all[:, hs]) * (DN_D ** -0.5) for hs in hsl]
    k_h = [_l2norm(k_all[:, hs]) for hs in hsl]
    beta_h = [beta_all[:, h:h + 1] for h in heads]
    k16 = [k.astype(BF16) for k in k_h]
    kk_h = [_dot_nt(k16[h], k16[h]) for h in heads]
    qk_h = [_dot_nt(q_h[h].astype(BF16), k16[h]) for h in heads]
    decay_h = []
    for h in heads:
        gcol = DN_HEADS + h
        dmat = gc_all[:, gcol:gcol + 1] - gc_t[gcol:gcol + 1, :]
        decay_h.append(jnp.exp(jnp.where(lower, dmat, NEG_BIG)))
    x_h = [jnp.where(strict, -(beta_h[h] * kk_h[h] * decay_h[h]), 0.0) for h in heads]
    attn_h = [(qk_h[h] * decay_h[h]).astype(BF16) for h in heads]
    pm_h = [eye + x for x in x_h]
    span = 2
    while span < tt:
        x16 = [x.astype(BF16) for x in x_h]
        x_h = [_dot(x16[h], x16[h]) for h in heads]
        pm_h = [pm_h[h] + _dot(x_h[h].astype(BF16), pm_h[h].astype(BF16)) for h in heads]
        span *= 2
    sol_h = []
    for h in heads:
        eg = eg_all[:, DN_HEADS + h:DN_HEADS + h + 1]
        kbeta = k_h[h] * beta_h[h]
        rhs = jnp.concatenate([v_all[:, hsl[h]] * beta_h[h], kbeta * eg], axis=1).astype(BF16)
        sol_h.append(_dot(pm_h[h].astype(BF16), rhs))
    r_h = []
    for h in heads:
        eg = eg_all[:, DN_HEADS + h:DN_HEADS + h + 1]
        wq = jnp.concatenate([sol_h[h][:, DN_D:], q_h[h] * eg], axis=0).astype(BF16)
        r_h.append(_dot(wq, s_ref[h].astype(BF16)))
    vnew_h = [(sol_h[h][:, :DN_D] - r_h[h][:tt]).astype(BF16) for h in heads]
    o_h = [r_h[h][tt:] + _dot(attn_h[h], vnew_h[h]) for h in heads]
    for h in heads:
        gcol = DN_HEADS + h
        kd = k_h[h] * egl_all[:, gcol:gcol + 1]
        s_ref[h] = s_ref[h] * glast_row[:, gcol:gcol + 1] + _dot(jnp.transpose(kd).astype(BF16), vnew_h[h])
        y = _rms(o_h[h], nw) * _silu(z_ref[0][:, hsl[h]])
        y_ref[0, :, hsl[h]] = y.astype(y_ref.dtype)

    @pl.when(j == nj - 1)
    def _():
        st_ref[0] = s_ref[...]


def gated_delta(p3, wc, par, nw, conv0, s0, *, tt, t_valid):
    b, t, _ = p3.shape
    hd = DN_HEADS * DN_D
    return pl.pallas_call(
        functools.partial(_dn_kernel, tt=tt, t_valid=t_valid, t_total=t),
        grid=(b, t // tt),
        in_specs=[pl.BlockSpec((1, tt, hd), lambda i, j: (i, j, 1)),
                  pl.BlockSpec((1, tt, hd), lambda i, j: (i, j, 2)),
                  pl.BlockSpec((1, tt, hd), lambda i, j: (i, j, 3)),
                  pl.BlockSpec((1, tt, hd), lambda i, j: (i, j, 5)),
                  pl.BlockSpec((1, tt, LANES), lambda i, j: (i, j, 3072 // LANES)),
                  pl.BlockSpec((CONV_W, 3 * hd), lambda i, j: (0, 0)),
                  pl.BlockSpec((SUBLANES, LANES), lambda i, j: (0, 0)),
                  pl.BlockSpec((1, DN_D), lambda i, j: (0, 0)),
                  pl.BlockSpec((1, SUBLANES, 3 * hd), lambda i, j: (i, 0, 0)),
                  pl.BlockSpec((1, DN_HEADS, DN_D, DN_D), lambda i, j: (i, 0, 0, 0))],
        out_specs=[pl.BlockSpec((1, tt, hd), lambda i, j: (i, j, 0)),
                   pl.BlockSpec((1, DN_HEADS, DN_D, DN_D), lambda i, j: (i, 0, 0, 0))],
        out_shape=[jax.ShapeDtypeStruct((b, t, hd), BF16),
                   jax.ShapeDtypeStruct((b, DN_HEADS, DN_D, DN_D), F32)],
        scratch_shapes=[pltpu.VMEM((SUBLANES, 3 * hd), F32),
                        pltpu.VMEM((DN_HEADS, DN_D, DN_D), F32)],
        compiler_params=_cparams("parallel", "arbitrary"),
        name="gated_delta",
    )(p3, p3, p3, p3, p3, wc, par, nw, conv0, s0)


LOW_KEY = INT_MIN + (1 << 23)


def _key_to_float(key):
    bits = key ^ ((key >> 31) & 0x7FFFFFFF)
    return pltpu.bitcast(bits, F32)


def _fold(x, axis, op):
    if axis == 0:
        return op.reduce(x.reshape(x.shape[0] // SUBLANES, SUBLANES, x.shape[1]), axis=0)
    parts = [x[:, s * LANES:(s + 1) * LANES] for s in range(x.shape[1] // LANES)]
    while len(parts) > 1:
        parts = [op(a, b) for a, b in zip(parts[::2], parts[1::2])] + ([parts[-1]] if len(parts) % 2 else [])
    return parts[0]


def _select_topk(sc_ref, thr_ref, n_slots, topk, pos_fn, axis, pos_bits):
    tile = sc_ref.shape[1:]
    acc_shape = (SUBLANES, tile[1]) if axis == 0 else (tile[0], LANES)
    vec_shape = (1, tile[1]) if axis == 0 else (tile[0], 1)

    def reduce_all(fn, op, init, dtype):
        def body(c, acc):
            return op(acc, _fold(fn(sc_ref[c], c), axis, op))

        acc = jnp.full(acc_shape, init, dtype)
        if isinstance(n_slots, int):
            for c in range(n_slots):
                acc = body(c, acc)
        else:
            acc = lax.fori_loop(0, n_slots, body, acc)
        return op.reduce(acc, axis=axis, keepdims=True)

    def count(pred):
        return reduce_all(lambda x, c: jnp.where(pred(x, c), 1, 0), jnp.add, 0, I32)

    def any_row(flag):
        return jnp.max(jnp.where(flag, 1.0, 0.0)) > 0.5

    def bit_body(bi, lo):
        cand = lo + lax.shift_left(jnp.int32(1), 31 - bi)
        cand_f = _key_to_float(cand)
        return jnp.where(count(lambda x, c: x >= cand_f) >= topk, cand, lo)

    lo = lax.fori_loop(0, 32, bit_body, jnp.full(vec_shape, INT_MIN, I32))
    lo = jnp.maximum(lo, LOW_KEY)
    thr0 = _key_to_float(lo)
    thr_ref[...] = thr0
    crowded = (lo > LOW_KEY) & (count(lambda x, c: x >= thr0) > topk)

    @pl.when(any_row(crowded))
    def _():
        def walk_cond(st):
            return st[3]

        def walk(st):
            upper, m_prev, c_prev, _ = st
            active = crowded & (c_prev < topk)
            m = reduce_all(lambda x, c: jnp.where((x >= thr0) & (x < upper), x, -jnp.inf),
                           jnp.maximum, -jnp.inf, F32)
            cnt = count(lambda x, c: x >= m)
            m = jnp.where(active, m, m_prev)
            cnt = jnp.where(active, cnt, c_prev)
            return jnp.where(active, m, upper), m, cnt, any_row(crowded & (cnt < topk))

        upper0 = _key_to_float(lo + 1)
        _, thr, cnt, _ = lax.while_loop(walk_cond, walk, (upper0, thr0, jnp.zeros(vec_shape, I32), True))
        thr = jnp.where(crowded, thr, thr0)
        thr_ref[...] = thr
        tie = crowded & (cnt > topk)

        @pl.when(any_row(tie))
        def _():
            need = topk - count(lambda x, c: x > thr)

            def pos_body(bi, j):
                cand = j + lax.shift_left(jnp.int32(1), pos_bits - 1 - bi)
                below = count(lambda x, c: (x == thr) & (pos_fn(c) < cand))
                return jnp.where(below < need, cand, j)

            jmax = lax.fori_loop(0, pos_bits, pos_body, jnp.zeros(vec_shape, I32))

            def fix(c, carry):
                x = sc_ref[c]
                drop = tie & (x == thr) & (pos_fn(c) > jmax)
                sc_ref[c] = jnp.where(drop, -jnp.inf, x)
                return carry

            if isinstance(n_slots, int):
                for c in range(n_slots):
                    fix(c, 0)
            else:
                lax.fori_loop(0, n_slots, fix, 0)


def _dsa_prompt_kernel(q_ref, qi_ref, wi_ref, k_ref, v_ref, ki_ref, o_ref,
                       sc_ref, thr_ref, vt_ref, qs_ref, qis_ref, si_ref, s_ref, m_ref, l_ref, acc_ref, *, qb, kc, topk):
    i = pl.program_id(1)
    s_len = k_ref.shape[1]
    nc = (i * qb + qb + kc - 1) // kc

    @pl.when(i == 0)
    def _():
        for c in range(s_len // kc):
            vt_ref[c] = jnp.transpose(v_ref[0, c * kc:(c + 1) * kc, :].astype(F32)).astype(BF16)

    q = q_ref[0] * (HEAD_DIM ** -0.5 * LOG2E)
    qi = qi_ref[0]
    for g in range(KV_HEADS):
        for r in range(GROUP):
            hh = g * GROUP + r
            qs_ref[g, r * qb:(r + 1) * qb, :] = q[:, hh * HEAD_DIM:(hh + 1) * HEAD_DIM].astype(BF16)
    for h in range(IDX_HEADS):
        qis_ref[h * qb:(h + 1) * qb, :] = qi[:, h * LANES:h * LANES + IDX_DIM]
    w_t = jnp.transpose(wi_ref[0])
    tpos = i * qb + lax.broadcasted_iota(I32, (kc, qb), 1)
    krow = lax.broadcasted_iota(I32, (kc, qb), 0)

    def index_logits(c):
        start = pl.multiple_of(c * kc, kc)
        keys = ki_ref[0, pl.ds(start, kc), :][:, :IDX_DIM]
        return _dot_nt(keys, qis_ref[...])

    si_ref[...] = index_logits(0)

    def score_body(c, carry):
        s_next = index_logits(jnp.minimum(c + 1, nc - 1))
        s_all = si_ref[...]
        sc = None
        for h in range(IDX_HEADS):
            term = w_t[h:h + 1, :] * jnp.maximum(s_all[:, h * qb:(h + 1) * qb], 0.0)
            sc = term if sc is None else sc + term
        sc_ref[c] = jnp.where(krow + c * kc <= tpos, sc, -jnp.inf)
        si_ref[...] = s_next
        return carry

    lax.fori_loop(0, nc, score_body, 0)

    _select_topk(sc_ref, thr_ref, nc, topk, pos_fn=lambda c: krow + c * kc, axis=0,
                 pos_bits=(s_len - 1).bit_length())
    thr = thr_ref[...]

    m_ref[...] = jnp.full(m_ref.shape, NEG_BIG, F32)
    l_ref[...] = jnp.zeros(l_ref.shape, F32)
    acc_ref[...] = jnp.zeros(acc_ref.shape, F32)

    def logits(c, g):
        start = pl.multiple_of(c * kc, kc)
        return _dot_nt(k_ref[0, pl.ds(start, kc), g * HEAD_DIM:(g + 1) * HEAD_DIM], qs_ref[g])

    def probs(g, s, bias_g):
        s = s + bias_g
        m_old = m_ref[g]
        m_new = jnp.maximum(m_old, jnp.max(s, axis=0, keepdims=True))
        p = jnp.exp2(s - m_new)
        alpha = jnp.exp2(m_old - m_new)
        l_ref[g] = alpha * l_ref[g] + jnp.sum(p, axis=0, keepdims=True)
        m_ref[g] = m_new
        return p.astype(BF16), alpha

    s_ref[...] = logits(0, 0)

    def attn_body(c, carry):
        s1 = logits(c, 1)
        s_next = logits(jnp.minimum(c + 1, nc - 1), 0)
        bias = jnp.where(sc_ref[c] >= thr, 0.0, NEG_BIG)
        bias_g = jnp.concatenate([bias] * GROUP, axis=1)
        vtc = vt_ref[c]
        p0, alpha0 = probs(0, s_ref[...], bias_g)
        p1, alpha1 = probs(1, s1, bias_g)
        acc_ref[0] = alpha0 * acc_ref[0] + _dot(vtc[:HEAD_DIM], p0)
        acc_ref[1] = alpha1 * acc_ref[1] + _dot(vtc[HEAD_DIM:], p1)
        s_ref[...] = s_next
        return carry

    lax.fori_loop(0, nc, attn_body, 0)
    for g in range(KV_HEADS):
        out_t = acc_ref[g] / l_ref[g]
        for r in range(GROUP):
            hh = g * GROUP + r
            o_ref[0, :, hh * HEAD_DIM:(hh + 1) * HEAD_DIM] = jnp.transpose(
                out_t[:, r * qb:(r + 1) * qb]).astype(o_ref.dtype)


ATT_Q0 = 0
ATT_QI0 = 1024
ATT_K0 = 2048
ATT_V0 = 2304
ATT_KI0 = 2560
ATT_WI0 = 2688
ATT_NP = 2816


def dsa_prompt(p3, pb3, *, qb, kc):
    b, s, _ = p3.shape
    topk = min(TOPK_MAX, s // 4)
    hw = ATT_HEADS * HEAD_DIM
    kvw = KV_HEADS * HEAD_DIM
    return pl.pallas_call(
        functools.partial(_dsa_prompt_kernel, qb=qb, kc=kc, topk=topk),
        grid=(b, s // qb),
        in_specs=[pl.BlockSpec((1, qb, hw), lambda bi, i: (bi, i, ATT_Q0 // hw)),
                  pl.BlockSpec((1, qb, IDX_HEADS * LANES), lambda bi, i: (bi, i, ATT_QI0 // (IDX_HEADS * LANES))),
                  pl.BlockSpec((1, qb, LANES), lambda bi, i: (bi, i, ATT_WI0 // LANES)),
                  pl.BlockSpec((1, s, kvw), lambda bi, i: (bi, 0, ATT_K0 // kvw)),
                  pl.BlockSpec((1, s, kvw), lambda bi, i: (bi, 0, ATT_V0 // kvw)),
                  pl.BlockSpec((1, s, LANES), lambda bi, i: (bi, 0, ATT_KI0 // LANES))],
        out_specs=pl.BlockSpec((1, qb, hw), lambda bi, i: (bi, i, 0)),
        out_shape=jax.ShapeDtypeStruct((b, s, hw), BF16),
        scratch_shapes=[pltpu.VMEM((s // kc, kc, qb), F32),
                        pltpu.VMEM((1, qb), F32),
                        pltpu.VMEM((s // kc, kvw, kc), BF16),
                        pltpu.VMEM((KV_HEADS, GROUP * qb, HEAD_DIM), BF16),
                        pltpu.VMEM((IDX_HEADS * qb, IDX_DIM), BF16),
                        pltpu.VMEM((kc, IDX_HEADS * qb), F32),
                        pltpu.VMEM((kc, GROUP * qb), F32),
                        pltpu.VMEM((KV_HEADS, 1, GROUP * qb), F32),
                        pltpu.VMEM((KV_HEADS, 1, GROUP * qb), F32),
                        pltpu.VMEM((KV_HEADS, HEAD_DIM, GROUP * qb), F32)],
        compiler_params=_cparams("parallel", "arbitrary"),
        name="dsa_prompt",
    )(p3, pb3, p3, pb3, pb3, pb3)


def _index_scores(qi, wi, keys, keys_transposed):
    sc = None
    for h in range(IDX_HEADS):
        qh = qi[:, h * LANES:h * LANES + IDX_DIM]
        s = _dot(qh, keys) if keys_transposed else _dot_nt(qh, keys)
        term = wi[:, h:h + 1] * jnp.maximum(s, 0.0)
        sc = term if sc is None else sc + term
    return sc


def _dsa_sample_scores_kernel(pt_ref, qi_ref, wi_ref, kin_ref, *rest, pages, nj, t_new):
    page_refs = rest[:pages]
    sc_ref = rest[pages]
    j = pl.program_id(1)
    rows = qi_ref.shape[1]
    width = pages * LANES
    qi = qi_ref[0].astype(BF16)
    wi = wi_ref[0]
    keys_t = jnp.concatenate([r[...] for r in page_refs], axis=1).astype(BF16)
    sc_ref[j] = _index_scores(qi, wi, keys_t, True)

    @pl.when(j == nj - 1)
    def _():
        kin = kin_ref[0][:, :IDX_DIM].astype(BF16)
        kin = jnp.concatenate([kin, jnp.zeros((LANES - rows, IDX_DIM), BF16)], axis=0)
        snew = _index_scores(qi, wi, kin, False)
        ri = lax.broadcasted_iota(I32, (rows, LANES), 0)
        ci = lax.broadcasted_iota(I32, (rows, LANES), 1)
        snew = jnp.where((ci <= ri) & (ci < t_new), snew, -jnp.inf)
        sc_ref[nj] = jnp.concatenate([snew, jnp.full((rows, width - LANES), -jnp.inf, F32)], axis=1)


def dsa_sample_scores(ps8, cache_kidx_t, layer, page_table, *, pages, t_new):
    db, rows, _ = ps8.shape
    n_pages = page_table.shape[1]
    page = cache_kidx_t.shape[3]
    nj = n_pages // pages
    width = pages * LANES

    def page_spec(r):
        return pl.BlockSpec((None, None, IDX_DIM, page),
                            lambda bi, j, pt: (layer, pt[bi * n_pages + j * pages + r], 0, 0))

    grid_spec = pltpu.PrefetchScalarGridSpec(
        num_scalar_prefetch=1,
        grid=(db, nj),
        in_specs=[pl.BlockSpec((1, rows, IDX_HEADS * LANES), lambda bi, j, pt: (bi, 0, ATT_QI0 // (IDX_HEADS * LANES))),
                  pl.BlockSpec((1, rows, LANES), lambda bi, j, pt: (bi, 0, ATT_WI0 // LANES)),
                  pl.BlockSpec((1, rows, LANES), lambda bi, j, pt: (bi, 0, ATT_KI0 // LANES))]
        + [page_spec(r) for r in range(pages)],
        out_specs=pl.BlockSpec((nj + 1, rows, width), lambda bi, j, pt: (0, bi, 0)),
    )
    return pl.pallas_call(
        functools.partial(_dsa_sample_scores_kernel, pages=pages, nj=nj, t_new=t_new),
        grid_spec=grid_spec,
        out_shape=jax.ShapeDtypeStruct((nj + 1, db * rows, width), F32),
        compiler_params=_cparams("parallel", "arbitrary"),
        name="dsa_sample_scores",
    )(page_table.reshape(-1), ps8, ps8, ps8, *([cache_kidx_t] * pages))


def _dsa_sample_select_kernel(s_in_ref, bias_ref, sc_ref, thr_ref, *, topk):
    slots, rows, width = sc_ref.shape
    sc_ref[...] = s_in_ref[...]
    lane = lax.broadcasted_iota(I32, (rows, width), 1)
    _select_topk(sc_ref, thr_ref, slots, topk, pos_fn=lambda c: lane + c * width, axis=1,
                 pos_bits=(slots * width - 1).bit_length())
    thr = thr_ref[...]
    for c in range(slots):
        bias_ref[c] = jnp.where(sc_ref[c] >= thr, 0.0, NEG_BIG)


def dsa_sample_select(scores, *, topk, tr):
    slots, r, width = scores.shape
    spec = pl.BlockSpec((slots, tr, width), lambda i: (0, i, 0))
    return pl.pallas_call(
        functools.partial(_dsa_sample_select_kernel, topk=topk),
        grid=(r // tr,),
        in_specs=[spec],
        out_specs=spec,
        out_shape=jax.ShapeDtypeStruct(scores.shape, F32),
        scratch_shapes=[pltpu.VMEM((slots, tr, width), F32), pltpu.VMEM((tr, 1), F32)],
        compiler_params=_cparams("parallel"),
        name="dsa_sample_select",
    )(scores)


def _softmax_update(s, m_old, l_old, acc_old, vals):
    m_new = jnp.maximum(m_old, jnp.max(s, axis=1, keepdims=True))
    p = jnp.exp(s - m_new)
    alpha = jnp.exp(m_old - m_new)
    l_new = alpha * l_old + jnp.sum(p, axis=1, keepdims=True)
    acc_new = alpha * acc_old + _dot(p.astype(BF16), vals)
    return m_new, l_new, acc_new


def _dsa_sample_attend_kernel(pt_ref, q_ref, kn_ref, vn_ref, bias_ref, *rest, pages):
    k_refs = rest[:pages]
    v_refs = rest[pages:2 * pages]
    o_ref, qs_ref, m_ref, l_ref, acc_ref = rest[2 * pages:]
    j = pl.program_id(1)
    nj = pl.num_programs(1)
    rows = q_ref.shape[1]
    page = k_refs[0].shape[0] // KV_HEADS

    @pl.when(j == 0)
    def _():
        q = q_ref[0] * (HEAD_DIM ** -0.5)
        for g in range(KV_HEADS):
            for r in range(GROUP):
                hh = g * GROUP + r
                qs_ref[g, r * rows:(r + 1) * rows, :] = q[:, hh * HEAD_DIM:(hh + 1) * HEAD_DIM].astype(BF16)
        m_ref[...] = jnp.full(m_ref.shape, NEG_BIG, F32)
        l_ref[...] = jnp.zeros(l_ref.shape, F32)
        acc_ref[...] = jnp.zeros(acc_ref.shape, F32)

    def update(g, kch, vch, bias4):
        s = _dot_nt(qs_ref[g], kch) + bias4
        m_new, l_new, acc_new = _softmax_update(s, m_ref[g], l_ref[g], acc_ref[g], vch)
        m_ref[g] = m_new
        l_ref[g] = l_new
        acc_ref[g] = acc_new

    bias4 = jnp.concatenate([bias_ref[j]] * GROUP, axis=0)
    for g in range(KV_HEADS):
        head_rows = pl.ds(g, page, stride=KV_HEADS)
        kch = jnp.concatenate([r[head_rows, :] for r in k_refs], axis=0).astype(BF16)
        vch = jnp.concatenate([r[head_rows, :] for r in v_refs], axis=0).astype(BF16)
        update(g, kch, vch, bias4)

    @pl.when(j == nj - 1)
    def _():
        pad = jnp.zeros((LANES - rows, HEAD_DIM), BF16)
        bias_new = jnp.concatenate([bias_ref[nj][:, :LANES]] * GROUP, axis=0)
        for g in range(KV_HEADS):
            gs = slice(g * HEAD_DIM, (g + 1) * HEAD_DIM)
            kn = jnp.concatenate([kn_ref[0][:, gs].astype(BF16), pad], axis=0)
            vn = jnp.concatenate([vn_ref[0][:, gs].astype(BF16), pad], axis=0)
            update(g, kn, vn, bias_new)
            out = acc_ref[g] / l_ref[g]
            for r in range(GROUP):
                hh = g * GROUP + r
                o_ref[0, :, hh * HEAD_DIM:(hh + 1) * HEAD_DIM] = out[r * rows:(r + 1) * rows].astype(o_ref.dtype)


def dsa_sample_attend(ps8, bias, cache_k, cache_v, layer, page_table, *, pages):
    db, rows, _ = ps8.shape
    n_pages = page_table.shape[1]
    page = cache_k.shape[2] // KV_HEADS
    nj = n_pages // pages
    width = pages * LANES
    hw = ATT_HEADS * HEAD_DIM
    kvw = KV_HEADS * HEAD_DIM

    def page_spec(r):
        return pl.BlockSpec((None, None, page * KV_HEADS, HEAD_DIM),
                            lambda bi, j, pt: (layer, pt[bi * n_pages + j * pages + r], 0, 0))

    grid_spec = pltpu.PrefetchScalarGridSpec(
        num_scalar_prefetch=1,
        grid=(db, nj),
        in_specs=[pl.BlockSpec((1, rows, hw), lambda bi, j, pt: (bi, 0, ATT_Q0 // hw)),
                  pl.BlockSpec((1, rows, kvw), lambda bi, j, pt: (bi, 0, ATT_K0 // kvw)),
                  pl.BlockSpec((1, rows, kvw), lambda bi, j, pt: (bi, 0, ATT_V0 // kvw)),
                  pl.BlockSpec((nj + 1, rows, width), lambda bi, j, pt: (0, bi, 0))]
        + [page_spec(r) for r in range(pages)] * 2,
        out_specs=pl.BlockSpec((1, rows, hw), lambda bi, j, pt: (bi, 0, 0)),
        scratch_shapes=[pltpu.VMEM((KV_HEADS, GROUP * rows, HEAD_DIM), BF16),
                        pltpu.VMEM((KV_HEADS, GROUP * rows, 1), F32),
                        pltpu.VMEM((KV_HEADS, GROUP * rows, 1), F32),
                        pltpu.VMEM((KV_HEADS, GROUP * rows, HEAD_DIM), F32)],
    )
    return pl.pallas_call(
        functools.partial(_dsa_sample_attend_kernel, pages=pages),
        grid_spec=grid_spec,
        out_shape=jax.ShapeDtypeStruct((db, rows, hw), BF16),
        compiler_params=_cparams("parallel", "arbitrary"),
        name="dsa_sample_attend",
    )(page_table.reshape(-1), ps8, ps8, ps8, bias, *([cache_k] * pages), *([cache_v] * pages))


REC_NP = 3200
SAMPLE_PAD_T = 128
SAMPLE_ROWS = 8
PAGES_PER_STEP = 16
SELECT_ROWS = 64


def _block_diag(w):
    h, a, b = w.shape
    return jnp.einsum("hab,hg->hagb", w, jnp.eye(h, dtype=w.dtype)).reshape(h * a, h * b)


def _rec_weights(w_in16, w_conv, b_conv, wa16, ba, wx16, bx, lam, a_log, dt_bias, dn_w, w_out16):
    c = ba.shape[0]
    par = jnp.zeros((SUBLANES, LANES), F32)
    par = par.at[0, DN_HEADS:2 * DN_HEADS].set(a_log).at[1, DN_HEADS:2 * DN_HEADS].set(dt_bias)
    return dict(
        w_in=jnp.pad(w_in16, ((0, 0), (0, REC_NP - w_in16.shape[1]))),
        wc_rg=w_conv[:, :c], wc_dn=w_conv[:, c:], bc=b_conv.reshape(1, c),
        wa=_block_diag(wa16), ba=ba.reshape(1, c), wx=_block_diag(wx16),
        bx=bx.reshape(1, c), lam=lam.reshape(1, c), par=par, nw=dn_w.reshape(1, -1),
        w_out_rg=w_out16[:c], w_out_dn=w_out16[c:])


def _att_weights(w_in16, w_out16):
    d = w_in16.shape[0]
    hw = ATT_HEADS * HEAD_DIM
    kvw = KV_HEADS * HEAD_DIM
    o1, o3 = hw, hw + 2 * kvw
    o4 = o3 + IDX_HEADS * IDX_DIM
    o5 = o4 + IDX_DIM
    qi = w_in16[:, o3:o4].reshape(d, IDX_HEADS, IDX_DIM)
    qi = jnp.pad(qi, ((0, 0), (0, 0), (0, LANES - IDX_DIM))).reshape(d, IDX_HEADS * LANES)
    ki = jnp.pad(w_in16[:, o4:o5], ((0, 0), (0, LANES - IDX_DIM)))
    wi = jnp.pad(w_in16[:, o5:], ((0, 0), (0, LANES - IDX_HEADS)))
    return dict(w_in=jnp.concatenate([w_in16[:, :o1], qi, w_in16[:, o1:o3], ki, wi], axis=1), w_out=w_out16)


def _rec_layer(x2d, b, t, wts, g, conv_state, h_state, s_state, *, tm, tt, first_at_zero, t_valid):
    c = wts["bc"].shape[1]
    p = norm_matmul(x2d, g, wts["w_in"], tm=tm)
    p3 = p.reshape(b, t_valid, REC_NP)
    conv_new = p3[:, t_valid - (CONV_W - 1):, :conv_state.shape[-1]]
    if t != t_valid:
        p3 = jnp.pad(p3, ((0, 0), (0, t - t_valid), (0, 0)))
    conv0 = jnp.pad(conv_state, ((0, 0), (SUBLANES - (CONV_W - 1), 0), (0, 0)))
    rg_y, h_t = rglru(p3, wts["wc_rg"], wts["bc"], wts["wa"], wts["ba"], wts["wx"], wts["bx"], wts["lam"],
                      conv0[:, :, :c], h_state.reshape(b, 1, c), tt=tt, first_at_zero=first_at_zero,
                      t_valid=t_valid)
    dn_y, s_t = gated_delta(p3, wts["wc_dn"], wts["par"], wts["nw"], conv0[:, :, c:], s_state,
                            tt=tt, t_valid=t_valid)
    if t != t_valid:
        rg_y = rg_y[:, :t_valid]
        dn_y = dn_y[:, :t_valid]
    acts = [rg_y.reshape(b * t_valid, -1), dn_y.reshape(b * t_valid, -1)]
    return acts, [wts["w_out_rg"], wts["w_out_dn"]], conv_new, h_t.reshape(b, c), s_t


def _split_kv(p3):
    b, t, _ = p3.shape
    kvw = KV_HEADS * HEAD_DIM
    k = p3[:, :, ATT_K0:ATT_K0 + kvw].reshape(b, t, KV_HEADS, HEAD_DIM)
    v = p3[:, :, ATT_V0:ATT_V0 + kvw].reshape(b, t, KV_HEADS, HEAD_DIM)
    ki = p3[:, :, ATT_KI0:ATT_KI0 + IDX_DIM]
    return k, v, ki


def kernel(x_prompt, x_sample, state_conv, state_h, state_S, cache_k, cache_v, cache_kidx, page_table,
           norm_mix, norm_mlp, norm_final, w_in_rec, w_conv, b_conv, rg_wa, rg_ba, rg_wx, rg_bx,
           rg_lambda, dn_a_log, dn_dt_bias, dn_norm, w_out_rec, w_in_att, w_out_att, w_up, w_down):
    bsz, seq, d = x_prompt.shape
    db, t_new, _ = x_sample.shape
    depth = norm_mix.shape[0]
    past = page_table.shape[1] * cache_k.shape[2]
    xp = x_prompt.reshape(bsz * seq, d)
    xs = x_sample.reshape(db * t_new, d)
    tm_p = 256
    tm_s = db * t_new
    cache_kidx_t = jnp.swapaxes(cache_kidx, 2, 3)
    cache_k4 = cache_k.reshape(cache_k.shape[:2] + (-1, HEAD_DIM))
    cache_v4 = cache_v.reshape(cache_v.shape[:2] + (-1, HEAD_DIM))
    w_up16 = cast_bf16(w_up)
    w_down16 = cast_bf16(w_down)
    w_in_rec16 = cast_bf16(w_in_rec)
    w_out_rec16 = cast_bf16(w_out_rec)
    rg_wa16 = cast_bf16(rg_wa)
    rg_wx16 = cast_bf16(rg_wx)
    w_in_att16 = cast_bf16(w_in_att)
    w_out_att16 = cast_bf16(w_out_att)

    conv_p, h_p, s_p, conv_s, h_s, s_s = [], [], [], [], [], []
    k_p, v_p, ki_p, k_s, v_s, ki_s = [], [], [], [], [], []
    for layer in range(depth):
        if layer % 2 == 0:
            r = layer // 2
            wts = _rec_weights(w_in_rec16[r], w_conv[r], b_conv[r], rg_wa16[r], rg_ba[r], rg_wx16[r], rg_bx[r],
                               rg_lambda[r], dn_a_log[r], dn_dt_bias[r], dn_norm[r], w_out_rec16[r])
            conv_ch = state_conv.shape[-1]
            acts_p, ws_p, cb, ht, st = _rec_layer(
                xp, bsz, seq, wts, norm_mix[layer], jnp.zeros((bsz, CONV_W - 1, conv_ch), F32),
                jnp.zeros((bsz, rg_ba.shape[1]), F32), jnp.zeros((bsz,) + state_S.shape[2:], F32),
                tm=tm_p, tt=256, first_at_zero=True, t_valid=seq)
            conv_p.append(cb); h_p.append(ht); s_p.append(st)
            acts_s, ws_s, cb, ht, st = _rec_layer(
                xs, db, SAMPLE_PAD_T, wts, norm_mix[layer], state_conv[r], state_h[r], state_S[r],
                tm=tm_s, tt=SAMPLE_PAD_T, first_at_zero=(past == 0), t_valid=t_new)
            conv_s.append(cb); h_s.append(ht); s_s.append(st)
        else:
            a = layer // 2
            wts = _att_weights(w_in_att16[a], w_out_att16[a])
            p, pb = norm_matmul(xp, norm_mix[layer], wts["w_in"], tm=tm_p, with_bf16=True)
            p3 = p.reshape(bsz, seq, ATT_NP)
            o = dsa_prompt(p3, pb.reshape(bsz, seq, ATT_NP), qb=128, kc=512)
            acts_p, ws_p = [o.reshape(bsz * seq, -1)], [wts["w_out"]]
            kk, vv, kii = _split_kv(p3)
            k_p.append(kk); v_p.append(vv); ki_p.append(kii)

            ps = norm_matmul(xs, norm_mix[layer], wts["w_in"], tm=tm_s)
            ps3 = ps.reshape(db, t_new, ATT_NP)
            ps8 = jnp.pad(ps3, ((0, 0), (0, SAMPLE_ROWS - t_new), (0, 0)))
            scores = dsa_sample_scores(ps8, cache_kidx_t, a, page_table, pages=PAGES_PER_STEP, t_new=t_new)
            bias = dsa_sample_select(scores, topk=min(TOPK_MAX, (past + t_new) // 4), tr=SELECT_ROWS)
            o = dsa_sample_attend(ps8, bias, cache_k4, cache_v4, a, page_table, pages=PAGES_PER_STEP)
            acts_s, ws_s = [o[:, :t_new].reshape(db * t_new, -1)], [wts["w_out"]]
            kk, vv, kii = _split_kv(ps3)
            k_s.append(kk); v_s.append(vv); ki_s.append(kii)
        xp = proj_mlp(acts_p, ws_p, xp, norm_mlp[layer], w_up16[layer], w_down16[layer], tm=512, tf=1024)
        xs = proj_mlp(acts_s, ws_s, xs, norm_mlp[layer], w_up16[layer], w_down16[layer], tm=tm_s, tf=1024)
    y_prompt = final_norm(xp, norm_final, tm=512).reshape(bsz, seq, d)
    y_sample = final_norm(xs, norm_final, tm=tm_s).reshape(db, t_new, d)
    return (y_prompt, y_sample,
            jnp.stack(conv_p), jnp.stack(h_p), jnp.stack(s_p),
            jnp.stack(k_p), jnp.stack(v_p), jnp.stack(ki_p),
            jnp.stack(conv_s), jnp.stack(h_s), jnp.stack(s_s),
            jnp.stack(k_s), jnp.stack(v_s), jnp.stack(ki_s))
```

```python
import functools

import jax
import jax.numpy as jnp
from jax import lax
from jax.experimental import pallas as pl
from jax.experimental.pallas import tpu as pltpu

F32 = jnp.float32
BF16 = jnp.bfloat16
I32 = jnp.int32

EPS = 1e-6
CONV_W = 4
D_RNN_HEADS = 8
RG_C = 8.0
DN_HEADS = 4
DN_D = 128
ATT_HEADS = 8
KV_HEADS = 2
GROUP = ATT_HEADS // KV_HEADS
HEAD_DIM = 128
IDX_HEADS = 8
IDX_DIM = 64
TOPK_MAX = 256
LANES = 128
SUBLANES = 8
VMEM_LIMIT = 48 * 1024 * 1024
INT_MIN = -2 ** 31
NEG_BIG = -1e30
LOG2E = 1.4426950408889634
CAST_TILE_ELEMS = 1 << 21


def _cparams(*sem):
    return pltpu.CompilerParams(dimension_semantics=sem, vmem_limit_bytes=VMEM_LIMIT)


def _sigmoid(x):
    return 1.0 / (1.0 + jnp.exp(-x))


def _softplus(x):
    return jnp.maximum(x, 0.0) + jnp.log(1.0 + jnp.exp(-jnp.abs(x)))


def _silu(x):
    return x * _sigmoid(x)


def _gelu_tanh(x):
    return 0.5 * x * (1.0 + jnp.tanh(0.7978845608028654 * (x + 0.044715 * x * x * x)))


def _rms(x, g):
    return x * lax.rsqrt(jnp.mean(x * x, axis=-1, keepdims=True) + EPS) * g


def _dot(a, b):
    return jnp.dot(a, b, preferred_element_type=F32)


def _dot_nt(a, b):
    return lax.dot_general(a, b, (((1,), (1,)), ((), ())), preferred_element_type=F32)


def _norm_matmul_kernel(x_ref, g_ref, w_ref, o_ref, *rest):
    xn = _rms(x_ref[...], g_ref[...]).astype(BF16)
    y = _dot(xn, w_ref[...])
    o_ref[...] = y
    if rest:
        rest[0][...] = y.astype(BF16)


def norm_matmul(x, g, w, *, tm, with_bf16=False):
    m, d = x.shape
    n = w.shape[1]
    out_shape = [jax.ShapeDtypeStruct((m, n), F32)]
    out_specs = [pl.BlockSpec((tm, n), lambda i: (i, 0))]
    if with_bf16:
        out_shape.append(jax.ShapeDtypeStruct((m, n), BF16))
        out_specs.append(pl.BlockSpec((tm, n), lambda i: (i, 0)))
    res = pl.pallas_call(
        _norm_matmul_kernel,
        grid=(m // tm,),
        in_specs=[pl.BlockSpec((tm, d), lambda i: (i, 0)),
                  pl.BlockSpec((1, d), lambda i: (0, 0)),
                  pl.BlockSpec((d, n), lambda i: (0, 0))],
        out_specs=out_specs,
        out_shape=out_shape,
        compiler_params=_cparams("parallel"),
        name="norm_matmul",
    )(x, g.reshape(1, d), w)
    return res if with_bf16 else res[0]


def _proj_mlp_kernel(*refs, n_in, final_norm):
    a_refs = refs[:n_in]
    w_refs = refs[n_in:2 * n_in]
    x_ref, g_ref, wu_ref, wd_ref = refs[2 * n_in:2 * n_in + 4]
    o_ref, xn_ref = refs[-2:]
    f = pl.program_id(1)

    @pl.when(f == 0)
    def _():
        x = x_ref[...]
        for a_ref, w_ref in zip(a_refs, w_refs):
            x = x + _dot(a_ref[...].astype(BF16), w_ref[...])
        xn_ref[...] = _rms(x, g_ref[...]).astype(BF16)
        o_ref[...] = x

    h = jnp.maximum(_dot(xn_ref[...], wu_ref[...]), 0.0)
    o_ref[...] += _dot((h * h).astype(BF16), wd_ref[...])
    if final_norm:
        gf_ref = refs[2 * n_in + 4]

        @pl.when(f == pl.num_programs(1) - 1)
        def _():
            o_ref[...] = _rms(o_ref[...], gf_ref[...])


def proj_mlp(acts, ws, x, g, w_up, w_down, *, tm, tf, final_g=None):
    m, d = x.shape
    dff = w_up.shape[1]
    n_in = len(acts)
    extra = [] if final_g is None else [final_g.reshape(1, d)]
    in_specs = [pl.BlockSpec((tm, a.shape[1]), lambda i, f: (i, 0)) for a in acts]
    in_specs += [pl.BlockSpec(w.shape, lambda i, f: (0, 0)) for w in ws]
    in_specs += [pl.BlockSpec((tm, d), lambda i, f: (i, 0)),
                 pl.BlockSpec((1, d), lambda i, f: (0, 0)),
                 pl.BlockSpec((d, tf), lambda i, f: (0, f)),
                 pl.BlockSpec((tf, d), lambda i, f: (f, 0))]
    in_specs += [pl.BlockSpec((1, d), lambda i, f: (0, 0)) for _ in extra]
    return pl.pallas_call(
        functools.partial(_proj_mlp_kernel, n_in=n_in, final_norm=bool(extra)),
        grid=(m // tm, dff // tf),
        in_specs=in_specs,
        out_specs=pl.BlockSpec((tm, d), lambda i, f: (i, 0)),
        out_shape=jax.ShapeDtypeStruct((m, d), F32),
        scratch_shapes=[pltpu.VMEM((tm, d), BF16)],
        compiler_params=_cparams("parallel", "arbitrary"),
        name="proj_mlp",
    )(*acts, *ws, x, g.reshape(1, d), w_up, w_down, *extra)


def _cast_kernel(x_ref, o_ref):
    o_ref[...] = x_ref[...].astype(o_ref.dtype)


def cast_bf16(w):
    n = w.shape[-1]
    w2 = w.reshape(-1, n)
    m = w2.shape[0]
    tm = m
    while tm * n > CAST_TILE_ELEMS and tm % (4 * SUBLANES) == 0:
        tm //= 2
    out = pl.pallas_call(
        _cast_kernel,
        grid=(m // tm,),
        in_specs=[pl.BlockSpec((tm, n), lambda i: (i, 0))],
        out_specs=pl.BlockSpec((tm, n), lambda i: (i, 0)),
        out_shape=jax.ShapeDtypeStruct((m, n), BF16),
        compiler_params=_cparams("parallel"),
        name="cast_bf16",
    )(w2)
    return out.reshape(w.shape)


def _causal_conv(x, prev, wc):
    tt, c = x.shape
    row = lax.broadcasted_iota(I32, (SUBLANES, c), 0)
    acc = x * wc[CONV_W - 1:CONV_W]
    for k in range(1, CONV_W):
        r = pltpu.roll(x, k, 0)
        top = jnp.where(row < k, pltpu.roll(prev, k, 0), r[:SUBLANES])
        r = jnp.concatenate([top, r[SUBLANES:]], axis=0)
        acc = acc + r * wc[CONV_W - 1 - k:CONV_W - k]
    return acc


def _rglru_kernel(u_ref, gate_ref, wc_ref, bc_ref, wa_ref, ba_ref, wx_ref, bx_ref, lam_ref,
                  conv0_ref, h0_ref, y_ref, ht_ref, carry_ref, h_ref, *, tt, first_at_zero, t_valid):
    j = pl.program_id(1)

    @pl.when(j == 0)
    def _():
        carry_ref[...] = conv0_ref[0]
        h_ref[...] = h0_ref[0]

    x = u_ref[0]
    c = x.shape[1]
    xr = _causal_conv(x, carry_ref[...], wc_ref[...]) + bc_ref[...]
    carry_ref[...] = x[tt - SUBLANES:]
    xb = xr.astype(BF16)
    r_g = _sigmoid(_dot(xb, wa_ref[...]) + ba_ref[...])
    i_g = _sigmoid(_dot(xb, wx_ref[...]) + bx_ref[...])
    log_a = -RG_C * r_g * _softplus(-lam_ref[...])
    a = jnp.exp(log_a)
    mult = jnp.sqrt(jnp.maximum(1.0 - jnp.exp(2.0 * log_a), 0.0))
    row = lax.broadcasted_iota(I32, (tt, c), 0)
    if first_at_zero:
        mult = jnp.where(row + j * tt == 0, 1.0, mult)
    b = xr * i_g * mult
    k = 1
    while k < tt:
        keep = row >= k
        a_s = jnp.where(keep, pltpu.roll(a, k, 0), 1.0)
        b_s = jnp.where(keep, pltpu.roll(b, k, 0), 0.0)
        b = a * b_s + b
        a = a * a_s
        k *= 2
    h = a * h_ref[...] + b
    h_ref[...] = h[tt - 1:tt]
    y_ref[0] = (h * _gelu_tanh(gate_ref[0])).astype(y_ref.dtype)
    j_last, r_last = divmod(t_valid - 1, tt)

    @pl.when(j == j_last)
    def _():
        ht_ref[0] = h[r_last:r_last + 1]


def rglru(p3, wc, bc, wa_bd, ba, wx_bd, bx, lam, conv0, h0, *, tt, first_at_zero, t_valid):
    b, t, _ = p3.shape
    c = wc.shape[1]
    vec = lambda: pl.BlockSpec((1, c), lambda i, j: (0, 0))
    return pl.pallas_call(
        functools.partial(_rglru_kernel, tt=tt, first_at_zero=first_at_zero, t_valid=t_valid),
        grid=(b, t // tt),
        in_specs=[pl.BlockSpec((1, tt, c), lambda i, j: (i, j, 0)),
                  pl.BlockSpec((1, tt, c), lambda i, j: (i, j, 2048 // c)),
                  pl.BlockSpec((CONV_W, c), lambda i, j: (0, 0)),
                  vec(),
                  pl.BlockSpec((c, c), lambda i, j: (0, 0)),
                  vec(),
                  pl.BlockSpec((c, c), lambda i, j: (0, 0)),
                  vec(), vec(),
                  pl.BlockSpec((1, SUBLANES, c), lambda i, j: (i, 0, 0)),
                  pl.BlockSpec((1, 1, c), lambda i, j: (i, 0, 0))],
        out_specs=[pl.BlockSpec((1, tt, c), lambda i, j: (i, j, 0)),
                   pl.BlockSpec((1, 1, c), lambda i, j: (i, 0, 0))],
        out_shape=[jax.ShapeDtypeStruct((b, t, c), BF16),
                   jax.ShapeDtypeStruct((b, 1, c), F32)],
        scratch_shapes=[pltpu.VMEM((SUBLANES, c), F32), pltpu.VMEM((1, c), F32)],
        compiler_params=_cparams("parallel", "arbitrary"),
        name="rglru",
    )(p3, p3, wc, bc, wa_bd, ba, wx_bd, bx, lam, conv0, h0)


def _split3(x):
    hi = x.astype(BF16)
    r1 = x - hi.astype(F32)
    mid = r1.astype(BF16)
    lo = (r1 - mid.astype(F32)).astype(BF16)
    return hi, mid, lo


def _l2norm(x):
    return x * lax.rsqrt(jnp.sum(x * x, axis=-1, keepdims=True) + EPS)


def _dn_kernel(uq_ref, uk_ref, uv_ref, z_ref, ba_ref, wc_ref, par_ref, nw_ref, conv0_ref, s0_ref,
               y_ref, st_ref, carry_ref, s_ref, *, tt, t_valid, t_total):
    j = pl.program_id(1)
    nj = pl.num_programs(1)
    hd = DN_HEADS * DN_D

    @pl.when(j == 0)
    def _():
        carry_ref[...] = conv0_ref[0]
        s_ref[...] = s0_ref[0]

    wc = wc_ref[...]
    prev = carry_ref[...]
    parts = []
    for n, ref in enumerate((uq_ref, uk_ref, uv_ref)):
        x = ref[0]
        sl = slice(n * hd, (n + 1) * hd)
        parts.append(_silu(_causal_conv(x, prev[:, sl], wc[:, sl])))
        carry_ref[:, sl] = x[tt - SUBLANES:]
    q_all, k_all, v_all = parts

    ba = ba_ref[0]
    par = par_ref[...]
    beta_all = _sigmoid(ba)
    g_all = -jnp.exp(par[0:1]) * _softplus(ba + par[1:2])
    if t_valid < t_total:
        live = lax.broadcasted_iota(I32, (tt, LANES), 0) + j * tt < t_valid
        beta_all = jnp.where(live, beta_all, 0.0)
        g_all = jnp.where(live, g_all, 0.0)

    ri = lax.broadcasted_iota(I32, (tt, tt), 0)
    ci = lax.broadcasted_iota(I32, (tt, tt), 1)
    lower = ri >= ci
    strict = ri > ci
    tri = jnp.where(lower, 1.0, 0.0).astype(BF16)
    eye = jnp.where(ri == ci, 1.0, 0.0)
    gc_all = sum(_dot(tri, part) for part in _split3(g_all))
    gc_t = jnp.transpose(gc_all)
    gl_row = gc_all[tt - 1:tt]
    eg_all = jnp.exp(gc_all)
    egl_all = jnp.exp(gl_row - gc_all)
    glast_row = jnp.exp(gl_row)
    nw = nw_ref[...]

    heads = range(DN_HEADS)
    hsl = [slice(h * DN_D, (h + 1) * DN_D) for h in heads]
    q_h = [_l2norm(q_all[:, hs]) * (DN_D ** -0.5) for hs in hsl]
    k_h = [_l2norm(k_all[:, hs]) for hs in hsl]
    beta_h = [beta_all[:, h:h + 1] for h in heads]
    k16 = [k.astype(BF16) for k in k_h]
    kk_h = [_dot_nt(k16[h], k16[h]) for h in heads]
    qk_h = [_dot_nt(q_h[h].astype(BF16), k16[h]) for h in heads]
    decay_h = []
    for h in heads:
        gcol = DN_HEADS + h
        dmat = gc_all[:, gcol:gcol + 1] - gc_t[gcol:gcol + 1, :]
        decay_h.append(jnp.exp(jnp.where(lower, dmat, NEG_BIG)))
    x_h = [jnp.where(strict, -(beta_h[h] * kk_h[h] * decay_h[h]), 0.0) for h in heads]
    attn_h = [(qk_h[h] * decay_h[h]).astype(BF16) for h in heads]
    pm_h = [eye + x for x in x_h]
    span = 2
    while span < tt:
        x16 = [x.astype(BF16) for x in x_h]
        x_h = [_dot(x16[h], x16[h]) for h in heads]
        pm_h = [pm_h[h] + _dot(x_h[h].astype(BF16), pm_h[h].astype(BF16)) for h in heads]
        span *= 2
    sol_h = []
    for h in heads:
        eg = eg_all[:, DN_HEADS + h:DN_HEADS + h + 1]
        kbeta = k_h[h] * beta_h[h]
        rhs = jnp.concatenate([v_all[:, hsl[h]] * beta_h[h], kbeta * eg], axis=1).astype(BF16)
        sol_h.append(_dot(pm_h[h].astype(BF16), rhs))
    r_h = []
    for h in heads:
        eg = eg_all[:, DN_HEADS + h:DN_HEADS + h + 1]
        wq = jnp.concatenate([sol_h[h][:, DN_D:], q_h[h] * eg], axis=0).astype(BF16)
        r_h.append(_dot(wq, s_ref[h].astype(BF16)))
    vnew_h = [(sol_h[h][:, :DN_D] - r_h[h][:tt]).astype(BF16) for h in heads]
    o_h = [r_h[h][tt:] + _dot(attn_h[h], vnew_h[h]) for h in heads]
    for h in heads:
        gcol = DN_HEADS + h
        kd = k_h[h] * egl_all[:, gcol:gcol + 1]
        s_ref[h] = s_ref[h] * glast_row[:, gcol:gcol + 1] + _dot(jnp.transpose(kd).astype(BF16), vnew_h[h])
        y = _rms(o_h[h], nw) * _silu(z_ref[0][:, hsl[h]])
        y_ref[0, :, hsl[h]] = y.astype(y_ref.dtype)

    @pl.when(j == nj - 1)
    def _():
        st_ref[0] = s_ref[...]


def gated_delta(p3, wc, par, nw, conv0, s0, *, tt, t_valid):
    b, t, _ = p3.shape
    hd = DN_HEADS * DN_D
    return pl.pallas_call(
        functools.partial(_dn_kernel, tt=tt, t_valid=t_valid, t_total=t),
        grid=(b, t // tt),
        in_specs=[pl.BlockSpec((1, tt, hd), lambda i, j: (i, j, 1)),
                  pl.BlockSpec((1, tt, hd), lambda i, j: (i, j, 2)),
                  pl.BlockSpec((1, tt, hd), lambda i, j: (i, j, 3)),
                  pl.BlockSpec((1, tt, hd), lambda i, j: (i, j, 5)),
                  pl.BlockSpec((1, tt, LANES), lambda i, j: (i, j, 3072 // LANES)),
                  pl.BlockSpec((CONV_W, 3 * hd), lambda i, j: (0, 0)),
                  pl.BlockSpec((SUBLANES, LANES), lambda i, j: (0, 0)),
                  pl.BlockSpec((1, DN_D), lambda i, j: (0, 0)),
                  pl.BlockSpec((1, SUBLANES, 3 * hd), lambda i, j: (i, 0, 0)),
                  pl.BlockSpec((1, DN_HEADS, DN_D, DN_D), lambda i, j: (i, 0, 0, 0))],
        out_specs=[pl.BlockSpec((1, tt, hd), lambda i, j: (i, j, 0)),
                   pl.BlockSpec((1, DN_HEADS, DN_D, DN_D), lambda i, j: (i, 0, 0, 0))],
        out_shape=[jax.ShapeDtypeStruct((b, t, hd), BF16),
                   jax.ShapeDtypeStruct((b, DN_HEADS, DN_D, DN_D), F32)],
        scratch_shapes=[pltpu.VMEM((SUBLANES, 3 * hd), F32),
                        pltpu.VMEM((DN_HEADS, DN_D, DN_D), F32)],
        compiler_params=_cparams("parallel", "arbitrary"),
        name="gated_delta",
    )(p3, p3, p3, p3, p3, wc, par, nw, conv0, s0)


LOW_KEY = INT_MIN + (1 << 23)


def _key_to_float(key):
    bits = key ^ ((key >> 31) & 0x7FFFFFFF)
    return pltpu.bitcast(bits, F32)


def _fold(x, axis, op):
    if axis == 0:
        return op.reduce(x.reshape(x.shape[0] // SUBLANES, SUBLANES, x.shape[1]), axis=0)
    parts = [x[:, s * LANES:(s + 1) * LANES] for s in range(x.shape[1] // LANES)]
    while len(parts) > 1:
        parts = [op(a, b) for a, b in zip(parts[::2], parts[1::2])] + ([parts[-1]] if len(parts) % 2 else [])
    return parts[0]


def _select_topk(sc_ref, thr_ref, n_slots, topk, pos_fn, axis, pos_bits):
    tile = sc_ref.shape[1:]
    acc_shape = (SUBLANES, tile[1]) if axis == 0 else (tile[0], LANES)
    vec_shape = (1, tile[1]) if axis == 0 else (tile[0], 1)

    def reduce_all(fn, op, init, dtype):
        def body(c, acc):
            return op(acc, _fold(fn(sc_ref[c], c), axis, op))

        acc = jnp.full(acc_shape, init, dtype)
        if isinstance(n_slots, int):
            for c in range(n_slots):
                acc = body(c, acc)
        else:
            acc = lax.fori_loop(0, n_slots, body, acc)
        return op.reduce(acc, axis=axis, keepdims=True)

    def count(pred):
        return reduce_all(lambda x, c: jnp.where(pred(x, c), 1, 0), jnp.add, 0, I32)

    def any_row(flag):
        return jnp.max(jnp.where(flag, 1.0, 0.0)) > 0.5

    def bit_body(bi, lo):
        cand = lo + lax.shift_left(jnp.int32(1), 31 - bi)
        cand_f = _key_to_float(cand)
        return jnp.where(count(lambda x, c: x >= cand_f) >= topk, cand, lo)

    lo = lax.fori_loop(0, 32, bit_body, jnp.full(vec_shape, INT_MIN, I32))
    lo = jnp.maximum(lo, LOW_KEY)
    thr0 = _key_to_float(lo)
    thr_ref[...] = thr0
    crowded = (lo > LOW_KEY) & (count(lambda x, c: x >= thr0) > topk)

    @pl.when(any_row(crowded))
    def _():
        def walk_cond(st):
            return st[3]

        def walk(st):
            upper, m_prev, c_prev, _ = st
            active = crowded & (c_prev < topk)
            m = reduce_all(lambda x, c: jnp.where((x >= thr0) & (x < upper), x, -jnp.inf),
                           jnp.maximum, -jnp.inf, F32)
            cnt = count(lambda x, c: x >= m)
            m = jnp.where(active, m, m_prev)
            cnt = jnp.where(active, cnt, c_prev)
            return jnp.where(active, m, upper), m, cnt, any_row(crowded & (cnt < topk))

        upper0 = _key_to_float(lo + 1)
        _, thr, cnt, _ = lax.while_loop(walk_cond, walk, (upper0, thr0, jnp.zeros(vec_shape, I32), True))
        thr = jnp.where(crowded, thr, thr0)
        thr_ref[...] = thr
        tie = crowded & (cnt > topk)

        @pl.when(any_row(tie))
        def _():
            need = topk - count(lambda x, c: x > thr)

            def pos_body(bi, j):
                cand = j + lax.shift_left(jnp.int32(1), pos_bits - 1 - bi)
                below = count(lambda x, c: (x == thr) & (pos_fn(c) < cand))
                return jnp.where(below < need, cand, j)

            jmax = lax.fori_loop(0, pos_bits, pos_body, jnp.zeros(vec_shape, I32))

            def fix(c, carry):
                x = sc_ref[c]
                drop = tie & (x == thr) & (pos_fn(c) > jmax)
                sc_ref[c] = jnp.where(drop, -jnp.inf, x)
                return carry

            if isinstance(n_slots, int):
                for c in range(n_slots):
                    fix(c, 0)
            else:
                lax.fori_loop(0, n_slots, fix, 0)


def _dsa_prompt_kernel(q_ref, qi_ref, wi_ref, k_ref, v_ref, ki_ref, o_ref,
                       sc_ref, thr_ref, vt_ref, qs_ref, qis_ref, si_ref, s_ref, m_ref, l_ref, acc_ref, *, qb, kc, topk):
    i = pl.program_id(1)
    s_len = k_ref.shape[1]
    nc = (i * qb + qb + kc - 1) // kc

    @pl.when(i == 0)
    def _():
        for c in range(s_len // kc):
            vt_ref[c] = jnp.transpose(v_ref[0, c * kc:(c + 1) * kc, :].astype(F32)).astype(BF16)

    q = q_ref[0] * (HEAD_DIM ** -0.5 * LOG2E)
    qi = qi_ref[0]
    for g in range(KV_HEADS):
        for r in range(GROUP):
            hh = g * GROUP + r
            qs_ref[g, r * qb:(r + 1) * qb, :] = q[:, hh * HEAD_DIM:(hh + 1) * HEAD_DIM].astype(BF16)
    for h in range(IDX_HEADS):
        qis_ref[h * qb:(h + 1) * qb, :] = qi[:, h * LANES:h * LANES + IDX_DIM]
    w_t = jnp.transpose(wi_ref[0])
    tpos = i * qb + lax.broadcasted_iota(I32, (kc, qb), 1)
    krow = lax.broadcasted_iota(I32, (kc, qb), 0)

    def index_logits(c):
        start = pl.multiple_of(c * kc, kc)
        keys = ki_ref[0, pl.ds(start, kc), :][:, :IDX_DIM]
        return _dot_nt(keys, qis_ref[...])

    si_ref[...] = index_logits(0)

    def score_body(c, carry):
        s_next = index_logits(jnp.minimum(c + 1, nc - 1))
        s_all = si_ref[...]
        sc = None
        for h in range(IDX_HEADS):
            term = w_t[h:h + 1, :] * jnp.maximum(s_all[:, h * qb:(h + 1) * qb], 0.0)
            sc = term if sc is None else sc + term
        sc_ref[c] = jnp.where(krow + c * kc <= tpos, sc, -jnp.inf)
        si_ref[...] = s_next
        return carry

    lax.fori_loop(0, nc, score_body, 0)

    _select_topk(sc_ref, thr_ref, nc, topk, pos_fn=lambda c: krow + c * kc, axis=0,
                 pos_bits=(s_len - 1).bit_length())
    thr = thr_ref[...]

    m_ref[...] = jnp.full(m_ref.shape, NEG_BIG, F32)
    l_ref[...] = jnp.zeros(l_ref.shape, F32)
    acc_ref[...] = jnp.zeros(acc_ref.shape, F32)

    def logits(c, g):
        start = pl.multiple_of(c * kc, kc)
        return _dot_nt(k_ref[0, pl.ds(start, kc), g * HEAD_DIM:(g + 1) * HEAD_DIM], qs_ref[g])

    def probs(g, s, bias_g):
        s = s + bias_g
        m_old = m_ref[g]
        m_new = jnp.maximum(m_old, jnp.max(s, axis=0, keepdims=True))
        p = jnp.exp2(s - m_new)
        alpha = jnp.exp2(m_old - m_new)
        l_ref[g] = alpha * l_ref[g] + jnp.sum(p, axis=0, keepdims=True)
        m_ref[g] = m_new
        return p.astype(BF16), alpha

    s_ref[...] = logits(0, 0)

    def attn_body(c, carry):
        s1 = logits(c, 1)
        s_next = logits(jnp.minimum(c + 1, nc - 1), 0)
        bias = jnp.where(sc_ref[c] >= thr, 0.0, NEG_BIG)
        bias_g = jnp.concatenate([bias] * GROUP, axis=1)
        vtc = vt_ref[c]
        p0, alpha0 = probs(0, s_ref[...], bias_g)
        p1, alpha1 = probs(1, s1, bias_g)
        acc_ref[0] = alpha0 * acc_ref[0] + _dot(vtc[:HEAD_DIM], p0)
        acc_ref[1] = alpha1 * acc_ref[1] + _dot(vtc[HEAD_DIM:], p1)
        s_ref[...] = s_next
        return carry

    lax.fori_loop(0, nc, attn_body, 0)
    for g in range(KV_HEADS):
        out_t = acc_ref[g] / l_ref[g]
        for r in range(GROUP):
            hh = g * GROUP + r
            o_ref[0, :, hh * HEAD_DIM:(hh + 1) * HEAD_DIM] = jnp.transpose(
                out_t[:, r * qb:(r + 1) * qb]).astype(o_ref.dtype)


ATT_Q0 = 0
ATT_QI0 = 1024
ATT_K0 = 2048
ATT_V0 = 2304
ATT_KI0 = 2560
ATT_WI0 = 2688
ATT_NP = 2816


def dsa_prompt(p3, pb3, *, qb, kc):
    b, s, _ = p3.shape
    topk = min(TOPK_MAX, s // 4)
    hw = ATT_HEADS * HEAD_DIM
    kvw = KV_HEADS * HEAD_DIM
    return pl.pallas_call(
        functools.partial(_dsa_prompt_kernel, qb=qb, kc=kc, topk=topk),
        grid=(b, s // qb),
        in_specs=[pl.BlockSpec((1, qb, hw), lambda bi, i: (bi, i, ATT_Q0 // hw)),
                  pl.BlockSpec((1, qb, IDX_HEADS * LANES), lambda bi, i: (bi, i, ATT_QI0 // (IDX_HEADS * LANES))),
                  pl.BlockSpec((1, qb, LANES), lambda bi, i: (bi, i, ATT_WI0 // LANES)),
                  pl.BlockSpec((1, s, kvw), lambda bi, i: (bi, 0, ATT_K0 // kvw)),
                  pl.BlockSpec((1, s, kvw), lambda bi, i: (bi, 0, ATT_V0 // kvw)),
                  pl.BlockSpec((1, s, LANES), lambda bi, i: (bi, 0, ATT_KI0 // LANES))],
        out_specs=pl.BlockSpec((1, qb, hw), lambda bi, i: (bi, i, 0)),
        out_shape=jax.ShapeDtypeStruct((b, s, hw), BF16),
        scratch_shapes=[pltpu.VMEM((s // kc, kc, qb), F32),
                        pltpu.VMEM((1, qb), F32),
                        pltpu.VMEM((s // kc, kvw, kc), BF16),
                        pltpu.VMEM((KV_HEADS, GROUP * qb, HEAD_DIM), BF16),
                        pltpu.VMEM((IDX_HEADS * qb, IDX_DIM), BF16),
                        pltpu.VMEM((kc, IDX_HEADS * qb), F32),
                        pltpu.VMEM((kc, GROUP * qb), F32),
                        pltpu.VMEM((KV_HEADS, 1, GROUP * qb), F32),
                        pltpu.VMEM((KV_HEADS, 1, GROUP * qb), F32),
                        pltpu.VMEM((KV_HEADS, HEAD_DIM, GROUP * qb), F32)],
        compiler_params=_cparams("parallel", "arbitrary"),
        name="dsa_prompt",
    )(p3, pb3, p3, pb3, pb3, pb3)


def _index_scores(qi, wi, keys, keys_transposed):
    sc = None
    for h in range(IDX_HEADS):
        qh = qi[:, h * LANES:h * LANES + IDX_DIM]
        s = _dot(qh, keys) if keys_transposed else _dot_nt(qh, keys)
        term = wi[:, h:h + 1] * jnp.maximum(s, 0.0)
        sc = term if sc is None else sc + term
    return sc


def _dsa_sample_scores_kernel(pt_ref, qi_ref, wi_ref, kin_ref, *rest, pages, nj, t_new):
    page_refs = rest[:pages]
    sc_ref = rest[pages]
    j = pl.program_id(1)
    rows = qi_ref.shape[1]
    width = pages * LANES
    qi = qi_ref[0].astype(BF16)
    wi = wi_ref[0]
    keys_t = jnp.concatenate([r[...] for r in page_refs], axis=1).astype(BF16)
    sc_ref[j] = _index_scores(qi, wi, keys_t, True)

    @pl.when(j == nj - 1)
    def _():
        kin = kin_ref[0][:, :IDX_DIM].astype(BF16)
        kin = jnp.concatenate([kin, jnp.zeros((LANES - rows, IDX_DIM), BF16)], axis=0)
        snew = _index_scores(qi, wi, kin, False)
        ri = lax.broadcasted_iota(I32, (rows, LANES), 0)
        ci = lax.broadcasted_iota(I32, (rows, LANES), 1)
        snew = jnp.where((ci <= ri) & (ci < t_new), snew, -jnp.inf)
        sc_ref[nj] = jnp.concatenate([snew, jnp.full((rows, width - LANES), -jnp.inf, F32)], axis=1)


def dsa_sample_scores(ps8, cache_kidx_t, layer, page_table, *, pages, t_new):
    db, rows, _ = ps8.shape
    n_pages = page_table.shape[1]
    page = cache_kidx_t.shape[3]
    nj = n_pages // pages
    width = pages * LANES

    def page_spec(r):
        return pl.BlockSpec((None, None, IDX_DIM, page),
                            lambda bi, j, pt: (layer, pt[bi * n_pages + j * pages + r], 0, 0))

    grid_spec = pltpu.PrefetchScalarGridSpec(
        num_scalar_prefetch=1,
        grid=(db, nj),
        in_specs=[pl.BlockSpec((1, rows, IDX_HEADS * LANES), lambda bi, j, pt: (bi, 0, ATT_QI0 // (IDX_HEADS * LANES))),
                  pl.BlockSpec((1, rows, LANES), lambda bi, j, pt: (bi, 0, ATT_WI0 // LANES)),
                  pl.BlockSpec((1, rows, LANES), lambda bi, j, pt: (bi, 0, ATT_KI0 // LANES))]
        + [page_spec(r) for r in range(pages)],
        out_specs=pl.BlockSpec((nj + 1, rows, width), lambda bi, j, pt: (0, bi, 0)),
    )
    return pl.pallas_call(
        functools.partial(_dsa_sample_scores_kernel, pages=pages, nj=nj, t_new=t_new),
        grid_spec=grid_spec,
        out_shape=jax.ShapeDtypeStruct((nj + 1, db * rows, width), F32),
        compiler_params=_cparams("parallel", "arbitrary"),
        name="dsa_sample_scores",
    )(page_table.reshape(-1), ps8, ps8, ps8, *([cache_kidx_t] * pages))


def _dsa_sample_select_kernel(s_in_ref, bias_ref, sc_ref, thr_ref, *, topk):
    slots, rows, width = sc_ref.shape
    sc_ref[...] = s_in_ref[...]
    lane = lax.broadcasted_iota(I32, (rows, width), 1)
    _select_topk(sc_ref, thr_ref, slots, topk, pos_fn=lambda c: lane + c * width, axis=1,
                 pos_bits=(slots * width - 1).bit_length())
    thr = thr_ref[...]
    for c in range(slots):
        bias_ref[c] = jnp.where(sc_ref[c] >= thr, 0.0, NEG_BIG)


def dsa_sample_select(scores, *, topk, tr):
    slots, r, width = scores.shape
    spec = pl.BlockSpec((slots, tr, width), lambda i: (0, i, 0))
    return pl.pallas_call(
        functools.partial(_dsa_sample_select_kernel, topk=topk),
        grid=(r // tr,),
        in_specs=[spec],
        out_specs=spec,
        out_shape=jax.ShapeDtypeStruct(scores.shape, F32),
        scratch_shapes=[pltpu.VMEM((slots, tr, width), F32), pltpu.VMEM((tr, 1), F32)],
        compiler_params=_cparams("parallel"),
        name="dsa_sample_select",
    )(scores)


def _softmax_update(s, m_old, l_old, acc_old, vals):
    m_new = jnp.maximum(m_old, jnp.max(s, axis=1, keepdims=True))
    p = jnp.exp(s - m_new)
    alpha = jnp.exp(m_old - m_new)
    l_new = alpha * l_old + jnp.sum(p, axis=1, keepdims=True)
    acc_new = alpha * acc_old + _dot(p.astype(BF16), vals)
    return m_new, l_new, acc_new


def _dsa_sample_attend_kernel(pt_ref, q_ref, kn_ref, vn_ref, bias_ref, *rest, pages):
    k_refs = rest[:pages]
    v_refs = rest[pages:2 * pages]
    o_ref, qs_ref, m_ref, l_ref, acc_ref = rest[2 * pages:]
    j = pl.program_id(1)
    nj = pl.num_programs(1)
    rows = q_ref.shape[1]
    page = k_refs[0].shape[0] // KV_HEADS

    @pl.when(j == 0)
    def _():
        q = q_ref[0] * (HEAD_DIM ** -0.5)
        for g in range(KV_HEADS):
            for r in range(GROUP):
                hh = g * GROUP + r
                qs_ref[g, r * rows:(r + 1) * rows, :] = q[:, hh * HEAD_DIM:(hh + 1) * HEAD_DIM].astype(BF16)
        m_ref[...] = jnp.full(m_ref.shape, NEG_BIG, F32)
        l_ref[...] = jnp.zeros(l_ref.shape, F32)
        acc_ref[...] = jnp.zeros(acc_ref.shape, F32)

    def update(g, kch, vch, bias4):
        s = _dot_nt(qs_ref[g], kch) + bias4
        m_new, l_new, acc_new = _softmax_update(s, m_ref[g], l_ref[g], acc_ref[g], vch)
        m_ref[g] = m_new
        l_ref[g] = l_new
        acc_ref[g] = acc_new

    bias4 = jnp.concatenate([bias_ref[j]] * GROUP, axis=0)
    for g in range(KV_HEADS):
        head_rows = pl.ds(g, page, stride=KV_HEADS)
        kch = jnp.concatenate([r[head_rows, :] for r in k_refs], axis=0).astype(BF16)
        vch = jnp.concatenate([r[head_rows, :] for r in v_refs], axis=0).astype(BF16)
        update(g, kch, vch, bias4)

    @pl.when(j == nj - 1)
    def _():
        pad = jnp.zeros((LANES - rows, HEAD_DIM), BF16)
        bias_new = jnp.concatenate([bias_ref[nj][:, :LANES]] * GROUP, axis=0)
        for g in range(KV_HEADS):
            gs = slice(g * HEAD_DIM, (g + 1) * HEAD_DIM)
            kn = jnp.concatenate([kn_ref[0][:, gs].astype(BF16), pad], axis=0)
            vn = jnp.concatenate([vn_ref[0][:, gs].astype(BF16), pad], axis=0)
            update(g, kn, vn, bias_new)
            out = acc_ref[g] / l_ref[g]
            for r in range(GROUP):
                hh = g * GROUP + r
                o_ref[0, :, hh * HEAD_DIM:(hh + 1) * HEAD_DIM] = out[r * rows:(r + 1) * rows].astype(o_ref.dtype)


def dsa_sample_attend(ps8, bias, cache_k, cache_v, layer, page_table, *, pages):
    db, rows, _ = ps8.shape
    n_pages = page_table.shape[1]
    page = cache_k.shape[2] // KV_HEADS
    nj = n_pages // pages
    width = pages * LANES
    hw = ATT_HEADS * HEAD_DIM
    kvw = KV_HEADS * HEAD_DIM

    def page_spec(r):
        return pl.BlockSpec((None, None, page * KV_HEADS, HEAD_DIM),
                            lambda bi, j, pt: (layer, pt[bi * n_pages + j * pages + r], 0, 0))

    grid_spec = pltpu.PrefetchScalarGridSpec(
        num_scalar_prefetch=1,
        grid=(db, nj),
        in_specs=[pl.BlockSpec((1, rows, hw), lambda bi, j, pt: (bi, 0, ATT_Q0 // hw)),
                  pl.BlockSpec((1, rows, kvw), lambda bi, j, pt: (bi, 0, ATT_K0 // kvw)),
                  pl.BlockSpec((1, rows, kvw), lambda bi, j, pt: (bi, 0, ATT_V0 // kvw)),
                  pl.BlockSpec((nj + 1, rows, width), lambda bi, j, pt: (0, bi, 0))]
        + [page_spec(r) for r in range(pages)] * 2,
        out_specs=pl.BlockSpec((1, rows, hw), lambda bi, j, pt: (bi, 0, 0)),
        scratch_shapes=[pltpu.VMEM((KV_HEADS, GROUP * rows, HEAD_DIM), BF16),
                        pltpu.VMEM((KV_HEADS, GROUP * rows, 1), F32),
                        pltpu.VMEM((KV_HEADS, GROUP * rows, 1), F32),
                        pltpu.VMEM((KV_HEADS, GROUP * rows, HEAD_DIM), F32)],
    )
    return pl.pallas_call(
        functools.partial(_dsa_sample_attend_kernel, pages=pages),
        grid_spec=grid_spec,
        out_shape=jax.ShapeDtypeStruct((db, rows, hw), BF16),
        compiler_params=_cparams("parallel", "arbitrary"),
        name="dsa_sample_attend",
    )(page_table.reshape(-1), ps8, ps8, ps8, bias, *([cache_k] * pages), *([cache_v] * pages))


REC_NP = 3200
SAMPLE_PAD_T = 128
SAMPLE_ROWS = 8
PAGES_PER_STEP = 16
SELECT_ROWS = 64


def _block_diag(w):
    h, a, b = w.shape
    return jnp.einsum("hab,hg->hagb", w, jnp.eye(h, dtype=w.dtype)).reshape(h * a, h * b)


def _rec_weights(w_in16, w_conv, b_conv, wa16, ba, wx16, bx, lam, a_log, dt_bias, dn_w, w_out16):
    c = ba.shape[0]
    par = jnp.zeros((SUBLANES, LANES), F32)
    par = par.at[0, DN_HEADS:2 * DN_HEADS].set(a_log).at[1, DN_HEADS:2 * DN_HEADS].set(dt_bias)
    return dict(
        w_in=jnp.pad(w_in16, ((0, 0), (0, REC_NP - w_in16.shape[1]))),
        wc_rg=w_conv[:, :c], wc_dn=w_conv[:, c:], bc=b_conv.reshape(1, c),
        wa=_block_diag(wa16), ba=ba.reshape(1, c), wx=_block_diag(wx16),
        bx=bx.reshape(1, c), lam=lam.reshape(1, c), par=par, nw=dn_w.reshape(1, -1),
        w_out_rg=w_out16[:c], w_out_dn=w_out16[c:])


def _att_weights(w_in16, w_out16):
    d = w_in16.shape[0]
    hw = ATT_HEADS * HEAD_DIM
    kvw = KV_HEADS * HEAD_DIM
    o1, o3 = hw, hw + 2 * kvw
    o4 = o3 + IDX_HEADS * IDX_DIM
    o5 = o4 + IDX_DIM
    qi = w_in16[:, o3:o4].reshape(d, IDX_HEADS, IDX_DIM)
    qi = jnp.pad(qi, ((0, 0), (0, 0), (0, LANES - IDX_DIM))).reshape(d, IDX_HEADS * LANES)
    ki = jnp.pad(w_in16[:, o4:o5], ((0, 0), (0, LANES - IDX_DIM)))
    wi = jnp.pad(w_in16[:, o5:], ((0, 0), (0, LANES - IDX_HEADS)))
    return dict(w_in=jnp.concatenate([w_in16[:, :o1], qi, w_in16[:, o1:o3], ki, wi], axis=1), w_out=w_out16)


def _rec_layer(x2d, b, t, wts, g, conv_state, h_state, s_state, *, tm, tt, first_at_zero, t_valid):
    c = wts["bc"].shape[1]
    p = norm_matmul(x2d, g, wts["w_in"], tm=tm)
    p3 = p.reshape(b, t_valid, REC_NP)
    conv_new = p3[:, t_valid - (CONV_W - 1):, :conv_state.shape[-1]]
    if t != t_valid:
        p3 = jnp.pad(p3, ((0, 0), (0, t - t_valid), (0, 0)))
    conv0 = jnp.pad(conv_state, ((0, 0), (SUBLANES - (CONV_W - 1), 0), (0, 0)))
    rg_y, h_t = rglru(p3, wts["wc_rg"], wts["bc"], wts["wa"], wts["ba"], wts["wx"], wts["bx"], wts["lam"],
                      conv0[:, :, :c], h_state.reshape(b, 1, c), tt=tt, first_at_zero=first_at_zero,
                      t_valid=t_valid)
    dn_y, s_t = gated_delta(p3, wts["wc_dn"], wts["par"], wts["nw"], conv0[:, :, c:], s_state,
                            tt=tt, t_valid=t_valid)
    if t != t_valid:
        rg_y = rg_y[:, :t_valid]
        dn_y = dn_y[:, :t_valid]
    acts = [rg_y.reshape(b * t_valid, -1), dn_y.reshape(b * t_valid, -1)]
    return acts, [wts["w_out_rg"], wts["w_out_dn"]], conv_new, h_t.reshape(b, c), s_t


def _split_kv(p3):
    b, t, _ = p3.shape
    kvw = KV_HEADS * HEAD_DIM
    k = p3[:, :, ATT_K0:ATT_K0 + kvw].reshape(b, t, KV_HEADS, HEAD_DIM)
    v = p3[:, :, ATT_V0:ATT_V0 + kvw].reshape(b, t, KV_HEADS, HEAD_DIM)
    ki = p3[:, :, ATT_KI0:ATT_KI0 + IDX_DIM]
    return k, v, ki


def kernel(x_prompt, x_sample, state_conv, state_h, state_S, cache_k, cache_v, cache_kidx, page_table,
           norm_mix, norm_mlp, norm_final, w_in_rec, w_conv, b_conv, rg_wa, rg_ba, rg_wx, rg_bx,
           rg_lambda, dn_a_log, dn_dt_bias, dn_norm, w_out_rec, w_in_att, w_out_att, w_up, w_down):
    bsz, seq, d = x_prompt.shape
    db, t_new, _ = x_sample.shape
    depth = norm_mix.shape[0]
    past = page_table.shape[1] * cache_k.shape[2]
    xp = x_prompt.reshape(bsz * seq, d)
    xs = x_sample.reshape(db * t_new, d)
    tm_p = 256
    tm_s = db * t_new
    cache_kidx_t = jnp.swapaxes(cache_kidx, 2, 3)
    cache_k4 = cache_k.reshape(cache_k.shape[:2] + (-1, HEAD_DIM))
    cache_v4 = cache_v.reshape(cache_v.shape[:2] + (-1, HEAD_DIM))
    w_up16 = cast_bf16(w_up)
    w_down16 = cast_bf16(w_down)
    w_in_rec16 = cast_bf16(w_in_rec)
    w_out_rec16 = cast_bf16(w_out_rec)
    rg_wa16 = cast_bf16(rg_wa)
    rg_wx16 = cast_bf16(rg_wx)
    w_in_att16 = cast_bf16(w_in_att)
    w_out_att16 = cast_bf16(w_out_att)

    conv_p, h_p, s_p, conv_s, h_s, s_s = [], [], [], [], [], []
    k_p, v_p, ki_p, k_s, v_s, ki_s = [], [], [], [], [], []
    for layer in range(depth):
        if layer % 2 == 0:
            r = layer // 2
            wts = _rec_weights(w_in_rec16[r], w_conv[r], b_conv[r], rg_wa16[r], rg_ba[r], rg_wx16[r], rg_bx[r],
                               rg_lambda[r], dn_a_log[r], dn_dt_bias[r], dn_norm[r], w_out_rec16[r])
            conv_ch = state_conv.shape[-1]
            acts_p, ws_p, cb, ht, st = _rec_layer(
                xp, bsz, seq, wts, norm_mix[layer], jnp.zeros((bsz, CONV_W - 1, conv_ch), F32),
                jnp.zeros((bsz, rg_ba.shape[1]), F32), jnp.zeros((bsz,) + state_S.shape[2:], F32),
                tm=tm_p, tt=256, first_at_zero=True, t_valid=seq)
            conv_p.append(cb); h_p.append(ht); s_p.append(st)
            acts_s, ws_s, cb, ht, st = _rec_layer(
                xs, db, SAMPLE_PAD_T, wts, norm_mix[layer], state_conv[r], state_h[r], state_S[r],
                tm=tm_s, tt=SAMPLE_PAD_T, first_at_zero=(past == 0), t_valid=t_new)
            conv_s.append(cb); h_s.append(ht); s_s.append(st)
        else:
            a = layer // 2
            wts = _att_weights(w_in_att16[a], w_out_att16[a])
            p, pb = norm_matmul(xp, norm_mix[layer], wts["w_in"], tm=tm_p, with_bf16=True)
            p3 = p.reshape(bsz, seq, ATT_NP)
            o = dsa_prompt(p3, pb.reshape(bsz, seq, ATT_NP), qb=128, kc=512)
            acts_p, ws_p = [o.reshape(bsz * seq, -1)], [wts["w_out"]]
            kk, vv, kii = _split_kv(p3)
            k_p.append(kk); v_p.append(vv); ki_p.append(kii)

            ps = norm_matmul(xs, norm_mix[layer], wts["w_in"], tm=tm_s)
            ps3 = ps.reshape(db, t_new, ATT_NP)
            ps8 = jnp.pad(ps3, ((0, 0), (0, SAMPLE_ROWS - t_new), (0, 0)))
            scores = dsa_sample_scores(ps8, cache_kidx_t, a, page_table, pages=PAGES_PER_STEP, t_new=t_new)
            bias = dsa_sample_select(scores, topk=min(TOPK_MAX, (past + t_new) // 4), tr=SELECT_ROWS)
            o = dsa_sample_attend(ps8, bias, cache_k4, cache_v4, a, page_table, pages=PAGES_PER_STEP)
            acts_s, ws_s = [o[:, :t_new].reshape(db * t_new, -1)], [wts["w_out"]]
            kk, vv, kii = _split_kv(ps3)
            k_s.append(kk); v_s.append(vv); ki_s.append(kii)
        final_g = norm_final if layer == depth - 1 else None
        xp = proj_mlp(acts_p, ws_p, xp, norm_mlp[layer], w_up16[layer], w_down16[layer], tm=512, tf=1024,
                      final_g=final_g)
        xs = proj_mlp(acts_s, ws_s, xs, norm_mlp[layer], w_up16[layer], w_down16[layer], tm=tm_s, tf=1024,
                      final_g=final_g)
    y_prompt = xp.reshape(bsz, seq, d)
    y_sample = xs.reshape(db, t_new, d)
    return (y_prompt, y_sample,
            jnp.stack(conv_p), jnp.stack(h_p), jnp.stack(s_p),
            jnp.stack(k_p), jnp.stack(v_p), jnp.stack(ki_p),
            jnp.stack(conv_s), jnp.stack(h_s), jnp.stack(s_s),
            jnp.stack(k_s), jnp.stack(v_s), jnp.stack(ki_s))
```
